```python
import jax, jax.numpy as jnp
from jax import lax
import numpy as np

D_MODEL = 1024
BATCH = 8
SEQ = 2048
DEPTH = 1

GRID_W = 64
CTX_LEN = 256
LRU_WIDTH = D_MODEL
LRU_HEADS = 8
LRU_HEAD_DIM = LRU_WIDTH // LRU_HEADS
CONV_WIDTH = 4
CONV_PAD_LEFT = 1
LRU_C = 8.0
SGU_WIDTH = D_MODEL
SGU_GROUPS = 8
SGU_GROUP_DIM = SGU_WIDTH // SGU_GROUPS
CHUNK = 128
D_MIX = LRU_WIDTH + SGU_WIDTH
D_IN = 2 * LRU_WIDTH + 3 * SGU_WIDTH
NORM_EPS = 1e-6
LN_EPS = 1e-5

kernel_name = "hybrid_rglru_chunk_sgu_dit_block"


def rmsnorm(x, g):
    xf = x.astype(jnp.float32)
    y = xf * lax.rsqrt(jnp.mean(xf * xf, axis=-1, keepdims=True) + NORM_EPS)
    return (y * g.astype(jnp.float32)).astype(x.dtype)


def ada_mod(cond, w, b):
    m = jax.nn.silu(cond) @ w + b
    return jnp.split(m, 3, axis=-1)


def project(h, shift, scale, norm_g, w_in):
    hn = rmsnorm(h, norm_g) * (1.0 + scale) + shift
    return hn @ w_in


def short_conv(xa, w, b):
    L = xa.shape[1]
    xp = jnp.pad(xa, ((0, 0), (CONV_PAD_LEFT, CONV_WIDTH - 1 - CONV_PAD_LEFT), (0, 0)))
    y = xp[:, 0:L] * w[0]
    for k in range(1, CONV_WIDTH):
        y = y + xp[:, k:k + L] * w[k]
    return y + b


def _lin_combine(e1, e2):
    a1, b1 = e1
    a2, b2 = e2
    return a1 * a2, a2 * b1 + b2


def rglru_direction(xc, h0, wa, ba, wx, bx, lam, reverse):
    Bn, L, _ = xc.shape
    xh = xc.reshape(Bn, L, LRU_HEADS, LRU_HEAD_DIM)
    r = jax.nn.sigmoid(jnp.einsum('blhi,hij->blhj', xh, wa) + ba).reshape(Bn, L, LRU_WIDTH)
    i = jax.nn.sigmoid(jnp.einsum('blhi,hij->blhj', xh, wx) + bx).reshape(Bn, L, LRU_WIDTH)
    log_a = -LRU_C * r * jax.nn.softplus(-lam.astype(jnp.float32))
    a = jnp.exp(log_a)
    u = jnp.sqrt(-jnp.expm1(2.0 * log_a)) * (i * xc)
    a_cum, h = lax.associative_scan(_lin_combine, (a, u), reverse=reverse, axis=1)
    h = h + a_cum * h0[:, None, :]
    final = h[:, 0] if reverse else h[:, -1]
    return h, final


def rglru_bidir(xc, h0f, h0b, wa, ba, wx, bx, lam):
    hf, ff = rglru_direction(xc, h0f, wa[0], ba[0], wx[0], bx[0], lam[0], False)
    hb, fb = rglru_direction(xc, h0b, wa[1], ba[1], wx[1], bx[1], lam[1], True)
    return hf + hb, ff, fb


def chunk_sgu(u, v, ln_g, ln_b, w_s, b_s, n_chunks):
    Bn = u.shape[0]
    vf = v.astype(jnp.float32)
    mu = jnp.mean(vf, axis=-1, keepdims=True)
    var = jnp.mean(jnp.square(vf - mu), axis=-1, keepdims=True)
    vn = (vf - mu) * lax.rsqrt(var + LN_EPS) * ln_g + ln_b
    vc = vn.reshape(Bn, n_chunks, CHUNK, SGU_GROUPS, SGU_GROUP_DIM)
    mixed = jnp.einsum('gpq,bnqgc->bnpgc', w_s, vc) + b_s.T[None, None, :, :, None]
    return u * mixed.reshape(Bn, n_chunks * CHUNK, SGU_WIDTH).astype(u.dtype)


def split_proj(z):
    W, S = LRU_WIDTH, SGU_WIDTH
    return (z[..., :W], z[..., W:2 * W], z[..., 2 * W:2 * W + S],
            z[..., 2 * W + S:2 * W + 2 * S], z[..., 2 * W + 2 * S:])


def mixer_out(y_lru, ga, y_sgu, gb, w_out):
    y = jnp.concatenate([y_lru * jax.nn.silu(ga), y_sgu * jax.nn.silu(gb)], axis=-1)
    return y @ w_out


def setup_inputs(seed: int = 0) -> dict:
    key = jax.random.key(seed)
    ks = jax.random.split(key, 24)
    nrm = jax.random.normal
    a_c = jax.random.uniform(ks[13], (DEPTH, 2, LRU_WIDTH), minval=0.9, maxval=0.999)
    s = a_c ** (1.0 / LRU_C)
    return {
        "x": nrm(ks[0], (BATCH, SEQ, D_MODEL)),
        "c": nrm(ks[1], (BATCH, D_MODEL)),
        "ctx": nrm(ks[2], (BATCH, CTX_LEN, D_MODEL)),
        "c_ctx": nrm(ks[3], (D_MODEL,)),
        "ada_w": nrm(ks[4], (DEPTH, D_MODEL, 3 * D_MODEL)) * (0.5 * D_MODEL ** -0.5),
        "ada_b": 0.01 * nrm(ks[5], (DEPTH, 3 * D_MODEL)),
        "norm_g": 1.0 + 0.05 * nrm(ks[6], (DEPTH, D_MODEL)),
        "w_in": nrm(ks[7], (DEPTH, D_MODEL, D_IN)) * D_MODEL ** -0.5,
        "conv_w": nrm(ks[8], (DEPTH, CONV_WIDTH, LRU_WIDTH)) * CONV_WIDTH ** -0.5,
        "conv_b": 0.01 * nrm(ks[9], (DEPTH, LRU_WIDTH)),
        "lru_wa": nrm(ks[10], (DEPTH, 2, LRU_HEADS, LRU_HEAD_DIM, LRU_HEAD_DIM)) * LRU_HEAD_DIM ** -0.5,
        "lru_ba": 0.01 * nrm(ks[11], (DEPTH, 2, LRU_HEADS, LRU_HEAD_DIM)),
        "lru_wx": nrm(ks[12], (DEPTH, 2, LRU_HEADS, LRU_HEAD_DIM, LRU_HEAD_DIM)) * LRU_HEAD_DIM ** -0.5,
        "lru_bx": 0.01 * nrm(ks[14], (DEPTH, 2, LRU_HEADS, LRU_HEAD_DIM)),
        "lru_lambda": jnp.log(s) - jnp.log1p(-s),
        "sgu_ln_g": 1.0 + 0.05 * nrm(ks[15], (DEPTH, SGU_WIDTH)),
        "sgu_ln_b": 0.01 * nrm(ks[16], (DEPTH, SGU_WIDTH)),
        "sgu_w": nrm(ks[17], (DEPTH, SGU_GROUPS, CHUNK, CHUNK)) * (0.5 * CHUNK ** -0.5),
        "sgu_b": 1.0 + 0.1 * nrm(ks[18], (DEPTH, SGU_GROUPS, CHUNK)),
        "w_out": nrm(ks[19], (DEPTH, D_MIX, D_MODEL)) * D_MIX ** -0.5,
        "final_g": 1.0 + 0.05 * nrm(ks[20], (D_MODEL,)),
    }


def reference(x, c, ctx, c_ctx, ada_w, ada_b, norm_g, w_in, conv_w, conv_b, lru_wa, lru_ba,
              lru_wx, lru_bx, lru_lambda, sgu_ln_g, sgu_ln_b, sgu_w, sgu_b, w_out, final_g):
    Bn, L, _ = x.shape
    rows = L // GRID_W
    n_chunks = rows * GRID_W // CHUNK
    n_ctx_chunks = ctx.shape[1] // CHUNK
    zeros = jnp.zeros((Bn, LRU_WIDTH), jnp.float32)
    for layer in range(DEPTH):
        sh_x, sc_x, g_x = ada_mod(c[:, None, :], ada_w[layer], ada_b[layer])
        sh_c, sc_c, g_c = ada_mod(c_ctx[None, None, :], ada_w[layer], ada_b[layer])
        lru_p = (lru_wa[layer], lru_ba[layer], lru_wx[layer], lru_bx[layer], lru_lambda[layer])
        last = layer == DEPTH - 1

        ctx_cols = LRU_WIDTH if last else D_IN
        zc = project(ctx, sh_c, sc_c, norm_g[layer], w_in[layer][:, :ctx_cols])
        xc_c = short_conv(zc[..., :LRU_WIDTH], conv_w[layer], conv_b[layer]).astype(jnp.float32)
        y_c, hf_c, hb_c = rglru_bidir(xc_c, zeros, zeros, *lru_p)

        zx = project(x, sh_x, sc_x, norm_g[layer], w_in[layer])
        xa_x, ga_x, u_x, v_x, gb_x = split_proj(zx)
        xc_x = short_conv(xa_x, conv_w[layer], conv_b[layer]).astype(jnp.float32)
        y_l, _, _ = rglru_bidir(xc_x, hf_c, hb_c, *lru_p)
        y_s = chunk_sgu(jax.nn.gelu(u_x), jax.nn.gelu(v_x), sgu_ln_g[layer], sgu_ln_b[layer],
                        sgu_w[layer], sgu_b[layer], n_chunks)
        x_new = x + g_x * mixer_out(y_l.astype(x.dtype), ga_x, y_s, gb_x, w_out[layer])

        if not last:
            _, ga_c, u_c, v_c, gb_c = split_proj(zc)
            y_sc = chunk_sgu(jax.nn.gelu(u_c), jax.nn.gelu(v_c), sgu_ln_g[layer], sgu_ln_b[layer],
                             sgu_w[layer], sgu_b[layer], n_ctx_chunks)
            ctx = ctx + g_c * mixer_out(y_c.astype(ctx.dtype), ga_c, y_sc, gb_c, w_out[layer])
        x = x_new
    return rmsnorm(x, final_g)
```

```python
import functools

import jax
import jax.numpy as jnp
from jax import lax
from jax.experimental import pallas as pl
from jax.experimental.pallas import tpu as pltpu

F32 = jnp.float32
BF16 = jnp.bfloat16

D_MODEL = 1024
BATCH = 8
LRU_HEADS = 8
HEAD_DIM = 128
SGU_GROUPS = 8
CHUNK = 128
LRU_C = 8.0
NORM_EPS = 1e-6
LN_EPS = 1e-5

LANES = 128
SUBLANES = 8
N_SLABS = D_MODEL // LANES
TL = CHUNK
RT = TL * BATCH
PITCH = TL + SUBLANES
HALO = SUBLANES
VMEM_LIMIT = 56 * 1024 * 1024


def _sigmoid(x):
    return 0.5 * jnp.tanh(0.5 * x) + 0.5


def _silu(x):
    return x * _sigmoid(x)


def _gelu_tanh(x):
    c = 0.7978845608028654
    return 0.5 * x * (1.0 + jnp.tanh(c * (x + 0.044715 * (x * x * x))))


def _norm_mod(x, g, sc, sh):
    ms = jnp.mean(x * x, axis=-1, keepdims=True)
    y = x * lax.rsqrt(ms + NORM_EPS) * g
    return y * (1.0 + sc[:, None, :]) + sh[:, None, :]


def _const_spec(shape):
    zeros = (0,) * len(shape)
    return pl.BlockSpec(shape, lambda j: zeros, pipeline_mode=pl.Buffered(1))


def _ada_kernel(cond_ref, w_ref, b_ref, o_ref):
    s = _silu(cond_ref[...])
    o_ref[...] = jnp.dot(s, w_ref[...], preferred_element_type=F32,
                         precision=lax.Precision.HIGHEST) + b_ref[...]


def _ada_call(cond, w, b):
    rows = cond.shape[0]
    n_out = w.shape[1]
    tn = D_MODEL
    return pl.pallas_call(
        _ada_kernel,
        grid=(n_out // tn,),
        in_specs=[pl.BlockSpec((rows, D_MODEL), lambda j: (0, 0)),
                  pl.BlockSpec((D_MODEL, tn), lambda j: (0, j)),
                  pl.BlockSpec((1, tn), lambda j: (0, j))],
        out_specs=pl.BlockSpec((rows, tn), lambda j: (0, j)),
        out_shape=jax.ShapeDtypeStruct((rows, n_out), F32),
        compiler_params=pltpu.CompilerParams(dimension_semantics=("arbitrary",)),
        name="ada_mod",
    )(cond, w, b)


def _pre_kernel(with_gate, x_ref, xp_ref, xn_ref, sc_ref, sh_ref, g_ref, w_ref, cw_ref, cb_ref,
                *rest):
    if with_gate:
        xc_ref, sga_ref, hn_s, halo_s, lhs_s, xa_s = rest
    else:
        xc_ref, hn_s, halo_s, lhs_s, xa_s = rest
    j = pl.program_id(0)
    n = pl.num_programs(0)
    g = g_ref[...]
    sc = sc_ref[...]
    sh = sh_ref[...]

    hn = _norm_mod(x_ref[...], g, sc, sh)
    hp = _norm_mod(xp_ref[...], g, sc, sh)
    hx = _norm_mod(xn_ref[...], g, sc, sh)
    for b in range(BATCH):
        for s in range(N_SLABS):
            cols = slice(s * LANES, (s + 1) * LANES)
            hn_s[s, b * PITCH:b * PITCH + TL, :] = hn[b, :, cols]
            halo_s[s, b * 2 * HALO:b * 2 * HALO + HALO, :] = hp[b, :, cols]
            halo_s[s, b * 2 * HALO + HALO:(b + 1) * 2 * HALO, :] = hx[b, :, cols]

    def build(t2, carry):
        for s in range(N_SLABS):
            v0 = hn_s[s, pl.ds(2 * t2, BATCH, stride=PITCH), :]
            v1 = hn_s[s, pl.ds(2 * t2 + 1, BATCH, stride=PITCH), :]
            row = pl.multiple_of(t2 * 2 * BATCH, 2 * BATCH)
            lhs_s[pl.ds(row, 2 * BATCH), s * LANES:(s + 1) * LANES] = (
                jnp.concatenate([v0, v1], axis=0).astype(BF16))
        return carry

    lax.fori_loop(0, TL // 2, build, 0)
    for s in range(N_SLABS):
        prev = halo_s[s, pl.ds(HALO - 1, BATCH, stride=2 * HALO), :]
        nxt0 = halo_s[s, pl.ds(HALO, BATCH, stride=2 * HALO), :]
        nxt1 = halo_s[s, pl.ds(HALO + 1, BATCH, stride=2 * HALO), :]
        lhs_s[RT:RT + 4 * BATCH, s * LANES:(s + 1) * LANES] = (
            jnp.concatenate([prev, nxt0, nxt1, nxt1], axis=0).astype(BF16))

    xa = jnp.dot(lhs_s[0:RT, :], w_ref[:, 0:D_MODEL], preferred_element_type=F32)
    xh = jnp.dot(lhs_s[RT:RT + 4 * BATCH, :], w_ref[:, 0:D_MODEL], preferred_element_type=F32)
    prev_valid = (j > 0).astype(F32)
    next_valid = (j < n - 1).astype(F32)
    xa_s[0:BATCH, :] = xh[0:BATCH, :] * prev_valid
    xa_s[BATCH:BATCH + RT, :] = xa
    xa_s[BATCH + RT:3 * BATCH + RT, :] = xh[BATCH:3 * BATCH, :] * next_valid

    cw = cw_ref[...]
    acc = xa_s[0:RT, :] * cw[0:1, :]
    for k in range(1, 4):
        acc = acc + xa_s[k * BATCH:k * BATCH + RT, :] * cw[k:k + 1, :]
    xc_ref[...] = acc + cb_ref[...]

    if with_gate:
        ga = jnp.dot(lhs_s[0:RT, :], w_ref[:, D_MODEL:2 * D_MODEL], preferred_element_type=F32)
        sga_ref[...] = _silu(ga).astype(BF16)


def _pre_call(x, sc, sh, norm_g, w, conv_w, conv_b, with_gate):
    _, seq, _ = x.shape
    n = seq // TL
    blocks_per_tile = TL // HALO
    n_halo_blocks = seq // HALO
    out_shape = [jax.ShapeDtypeStruct((seq * BATCH, D_MODEL), F32)]
    out_specs = [pl.BlockSpec((RT, D_MODEL), lambda j: (j, 0))]
    if with_gate:
        out_shape.append(jax.ShapeDtypeStruct((seq * BATCH, D_MODEL), BF16))
        out_specs.append(pl.BlockSpec((RT, D_MODEL), lambda j: (j, 0)))
    return pl.pallas_call(
        functools.partial(_pre_kernel, with_gate),
        grid=(n,),
        in_specs=[
            pl.BlockSpec((BATCH, TL, D_MODEL), lambda j: (0, j, 0)),
            pl.BlockSpec((BATCH, HALO, D_MODEL),
                         lambda j: (0, jnp.maximum(j * blocks_per_tile - 1, 0), 0)),
            pl.BlockSpec((BATCH, HALO, D_MODEL),
                         lambda j: (0, jnp.minimum((j + 1) * blocks_per_tile, n_halo_blocks - 1), 0)),
            _const_spec((BATCH, D_MODEL)),
            _const_spec((BATCH, D_MODEL)),
            _const_spec((1, D_MODEL)),
            _const_spec(w.shape),
            _const_spec((4, D_MODEL)),
            _const_spec((1, D_MODEL)),
        ],
        out_specs=out_specs,
        out_shape=out_shape,
        scratch_shapes=[
            pltpu.VMEM((N_SLABS, BATCH * PITCH, LANES), F32),
            pltpu.VMEM((N_SLABS, BATCH * 2 * HALO, LANES), F32),
            pltpu.VMEM((RT + 4 * BATCH, D_MODEL), BF16),
            pltpu.VMEM((RT + 3 * BATCH, D_MODEL), F32),
        ],
        compiler_params=pltpu.CompilerParams(dimension_semantics=("arbitrary",),
                                             vmem_limit_bytes=VMEM_LIMIT),
        name="lru_pre_gate" if with_gate else "lru_pre",
    )(x, x, x, sc, sh, norm_g, w, conv_w, conv_b)


def _scan_kernel(reverse, out_h, combine, xc_ref, wa_ref, wx_ref, ba_ref, bx_ref, lam_ref, h0_ref,
                 *rest):
    rest = list(rest)
    if combine:
        hb_ref = rest.pop(0)
        sga_ref = rest.pop(0)
    hfin_ref = rest.pop(0)
    if out_h:
        h_ref = rest.pop(0)
    if combine:
        y_ref = rest.pop(0)
    carry_s = rest.pop(0)
    a_s = rest.pop(0)
    u_s = rest.pop(0)
    if combine:
        rl_s = rest.pop(0)

    j = pl.program_id(0)

    @pl.when(j == 0)
    def _():
        carry_s[...] = h0_ref[...]

    z = -lam_ref[...]
    coef = -LRU_C * (jnp.maximum(z, 0.0) + jnp.log1p(jnp.exp(-jnp.abs(z))))
    for h in range(LRU_HEADS):
        cols = slice(h * HEAD_DIM, (h + 1) * HEAD_DIM)
        xh = xc_ref[:, cols]
        xb = xh.astype(BF16)
        r = _sigmoid(jnp.dot(xb, wa_ref[h], preferred_element_type=F32) + ba_ref[:, cols])
        i = _sigmoid(jnp.dot(xb, wx_ref[h], preferred_element_type=F32) + bx_ref[:, cols])
        a = jnp.exp(coef[:, cols] * r)
        a_s[:, cols] = a
        u_s[:, cols] = jnp.sqrt(1.0 - a * a) * (i * xh)

    def step(k, hcur):
        t = (TL - 1 - k) if reverse else k
        row = pl.multiple_of(t * BATCH, BATCH)
        hnew = a_s[pl.ds(row, BATCH), :] * hcur + u_s[pl.ds(row, BATCH), :]
        u_s[pl.ds(row, BATCH), :] = hnew
        return hnew

    hlast = lax.fori_loop(0, TL, step, carry_s[...], unroll=8)
    carry_s[...] = hlast
    hfin_ref[...] = hlast

    if out_h:
        h_ref[...] = u_s[...]
    if combine:
        for s in range(N_SLABS):
            cols = slice(s * LANES, (s + 1) * LANES)
            rl_s[s] = (u_s[:, cols] + hb_ref[:, cols]) * sga_ref[:, cols].astype(F32)
        for b in range(BATCH):
            for s in range(N_SLABS):
                y_ref[b, :, s * LANES:(s + 1) * LANES] = (
                    rl_s[s, pl.ds(b, TL, stride=BATCH), :].astype(BF16))


def _scan_call(xc, wa, wx, ba, bx, lam, h0, hb=None, sga=None, *, reverse, out_h, combine):
    rows = xc.shape[0]
    seq = rows // BATCH
    n = seq // TL
    tile = (lambda j: (n - 1 - j, 0)) if reverse else (lambda j: (j, 0))
    in_specs = [
        pl.BlockSpec((RT, D_MODEL), tile),
        _const_spec((LRU_HEADS, HEAD_DIM, HEAD_DIM)),
        _const_spec((LRU_HEADS, HEAD_DIM, HEAD_DIM)),
        _const_spec((1, D_MODEL)),
        _const_spec((1, D_MODEL)),
        _const_spec((1, D_MODEL)),
        _const_spec((BATCH, D_MODEL)),
    ]
    args = [xc, wa, wx, ba, bx, lam, h0]
    if combine:
        in_specs += [pl.BlockSpec((RT, D_MODEL), tile), pl.BlockSpec((RT, D_MODEL), tile)]
        args += [hb, sga]
    out_shape = [jax.ShapeDtypeStruct((BATCH, D_MODEL), F32)]
    out_specs = [pl.BlockSpec((BATCH, D_MODEL), lambda j: (0, 0))]
    if out_h:
        out_shape.append(jax.ShapeDtypeStruct((rows, D_MODEL), F32))
        out_specs.append(pl.BlockSpec((RT, D_MODEL), tile))
    if combine:
        out_shape.append(jax.ShapeDtypeStruct((BATCH, seq, D_MODEL), BF16))
        out_specs.append(pl.BlockSpec((BATCH, TL, D_MODEL),
                                      (lambda j: (0, n - 1 - j, 0)) if reverse else (lambda j: (0, j, 0))))
    scratch = [pltpu.VMEM((BATCH, D_MODEL), F32),
               pltpu.VMEM((RT, D_MODEL), F32),
               pltpu.VMEM((RT, D_MODEL), F32)]
    if combine:
        scratch.append(pltpu.VMEM((N_SLABS, RT, LANES), F32))
    return pl.pallas_call(
        functools.partial(_scan_kernel, reverse, out_h, combine),
        grid=(n,),
        in_specs=in_specs,
        out_specs=out_specs,
        out_shape=out_shape,
        scratch_shapes=scratch,
        compiler_params=pltpu.CompilerParams(dimension_semantics=("arbitrary",),
                                             vmem_limit_bytes=VMEM_LIMIT),
        name="lru_scan_" + ("bwd" if reverse else "fwd") + ("_combine" if combine else ""),
    )(*args)


def _final_kernel(x_ref, sc_ref, sh_ref, gx_ref, g_ref, w_ref, lng_ref, lnb_ref, ws_ref, bs_ref,
                  yl_ref, wo_ref, fg_ref, o_ref, vn_s, y_s):
    x = x_ref[...]
    hn = _norm_mod(x, g_ref[...], sc_ref[...], sh_ref[...])
    lhs = hn.reshape(RT, D_MODEL).astype(BF16)

    v = _gelu_tanh(jnp.dot(lhs, w_ref[:, D_MODEL:2 * D_MODEL], preferred_element_type=F32))
    mu = jnp.mean(v, axis=-1, keepdims=True)
    vc = v - mu
    var = jnp.mean(vc * vc, axis=-1, keepdims=True)
    vn_s[...] = (vc * lax.rsqrt(var + LN_EPS) * lng_ref[...] + lnb_ref[...]).astype(BF16)

    u = _gelu_tanh(jnp.dot(lhs, w_ref[:, 0:D_MODEL], preferred_element_type=F32))
    gb = _silu(jnp.dot(lhs, w_ref[:, 2 * D_MODEL:3 * D_MODEL], preferred_element_type=F32))
    ug = u * gb

    y_s[:, 0:D_MODEL] = yl_ref[...].reshape(RT, D_MODEL)
    for b in range(BATCH):
        rows = slice(b * TL, (b + 1) * TL)
        for gidx in range(SGU_GROUPS):
            cols = slice(gidx * LANES, (gidx + 1) * LANES)
            mixed = jnp.dot(ws_ref[gidx], vn_s[rows, cols], preferred_element_type=F32) + bs_ref[gidx]
            y_s[rows, D_MODEL + gidx * LANES:D_MODEL + (gidx + 1) * LANES] = (
                ug[rows, cols] * mixed).astype(BF16)

    mix = jnp.dot(y_s[...], wo_ref[...], preferred_element_type=F32)
    xnew = x + gx_ref[...][:, None, :] * mix.reshape(BATCH, TL, D_MODEL)
    ms = jnp.mean(xnew * xnew, axis=-1, keepdims=True)
    o_ref[...] = xnew * lax.rsqrt(ms + NORM_EPS) * fg_ref[...]


def _final_call(x, sc, sh, gx, norm_g, w_uvg, ln_g, ln_b, w_s, b_s, ylg, w_out, final_g):
    _, seq, _ = x.shape
    n = seq // TL
    tile = pl.BlockSpec((BATCH, TL, D_MODEL), lambda j: (0, j, 0))
    return pl.pallas_call(
        _final_kernel,
        grid=(n,),
        in_specs=[
            tile,
            _const_spec((BATCH, D_MODEL)),
            _const_spec((BATCH, D_MODEL)),
            _const_spec((BATCH, D_MODEL)),
            _const_spec((1, D_MODEL)),
            _const_spec(w_uvg.shape),
            _const_spec((1, D_MODEL)),
            _const_spec((1, D_MODEL)),
            _const_spec((SGU_GROUPS, CHUNK, CHUNK)),
            _const_spec((SGU_GROUPS, CHUNK, LANES)),
            tile,
            _const_spec(w_out.shape),
            _const_spec((1, D_MODEL)),
        ],
        out_specs=tile,
        out_shape=jax.ShapeDtypeStruct(x.shape, F32),
        scratch_shapes=[pltpu.VMEM((RT, D_MODEL), BF16),
                        pltpu.VMEM((RT, 2 * D_MODEL), BF16)],
        compiler_params=pltpu.CompilerParams(dimension_semantics=("arbitrary",),
                                             vmem_limit_bytes=VMEM_LIMIT),
        name="sgu_out_final",
    )(x, sc, sh, gx, norm_g, w_uvg, ln_g, ln_b, w_s, b_s, ylg, w_out, final_g)


def kernel(x, c, ctx, c_ctx, ada_w, ada_b, norm_g, w_in, conv_w, conv_b, lru_wa, lru_ba,
           lru_wx, lru_bx, lru_lambda, sgu_ln_g, sgu_ln_b, sgu_w, sgu_b, w_out, final_g):
    assert x.shape == (BATCH, x.shape[1], D_MODEL) and x.shape[1] % TL == 0
    assert ctx.shape == (BATCH, ctx.shape[1], D_MODEL) and ctx.shape[1] % TL == 0
    assert ada_w.shape[0] == 1, "single-layer block"
    layer = 0
    W = D_MODEL

    cond = jnp.concatenate([c, c_ctx[None, :], jnp.zeros((BATCH - 1, D_MODEL), F32)], axis=0)
    mod = _ada_call(cond, ada_w[layer], ada_b[layer][None, :])
    sh_x, sc_x, g_x = mod[:BATCH, :W], mod[:BATCH, W:2 * W], mod[:BATCH, 2 * W:]
    sh_c = jnp.broadcast_to(mod[BATCH:BATCH + 1, :W], (BATCH, W))
    sc_c = jnp.broadcast_to(mod[BATCH:BATCH + 1, W:2 * W], (BATCH, W))

    ng = norm_g[layer][None, :]
    w_lru = w_in[layer][:, :2 * W].astype(BF16)
    w_uvg = w_in[layer][:, 2 * W:].astype(BF16)
    cw = conv_w[layer]
    cb = conv_b[layer][None, :]
    wa = lru_wa[layer].astype(BF16)
    wx = lru_wx[layer].astype(BF16)
    ba = lru_ba[layer].reshape(2, 1, W)
    bx = lru_bx[layer].reshape(2, 1, W)
    lam = lru_lambda[layer].reshape(2, 1, W)
    zeros = jnp.zeros((BATCH, W), F32)

    def scan(xc, d, h0, **kw):
        return _scan_call(xc, wa[d], wx[d], ba[d], bx[d], lam[d], h0, reverse=(d == 1), **kw)

    (xc_c,) = _pre_call(ctx, sc_c, sh_c, ng, w_lru[:, :W], cw, cb, with_gate=False)
    (hb_c,) = scan(xc_c, 1, zeros, out_h=False, combine=False)
    (hf_c,) = scan(xc_c, 0, zeros, out_h=False, combine=False)

    xc, sga = _pre_call(x, sc_x, sh_x, ng, w_lru, cw, cb, with_gate=True)
    _, hb = scan(xc, 1, hb_c, out_h=True, combine=False)
    _, ylg = scan(xc, 0, hf_c, hb=hb, sga=sga, out_h=False, combine=True)

    bs = jnp.broadcast_to(sgu_b[layer][:, :, None], (SGU_GROUPS, CHUNK, LANES))
    return _final_call(x, sc_x, sh_x, g_x, ng, w_uvg, sgu_ln_g[layer][None, :],
                       sgu_ln_b[layer][None, :], sgu_w[layer].astype(BF16), bs, ylg,
                       w_out[layer].astype(BF16), final_g[None, :])
```

```python
import functools
import math

import jax
import jax.numpy as jnp
from jax import lax
from jax.experimental import pallas as pl
from jax.experimental.pallas import tpu as pltpu

F32 = jnp.float32
BF16 = jnp.bfloat16

D_MODEL = 1024
BATCH = 8
LRU_HEADS = 8
HEAD_DIM = 128
SGU_GROUPS = 8
CHUNK = 128
LRU_C = 8.0
NORM_EPS = 1e-6
LN_EPS = 1e-5

LANES = 128
SUBLANES = 8
N_SLABS = D_MODEL // LANES
TL = CHUNK
RT = TL * BATCH
PITCH = TL + SUBLANES
HALO = SUBLANES
VMEM_LIMIT = 56 * 1024 * 1024


def _silu(x):
    h = 0.5 * x
    return h * jnp.tanh(h) + h


def _gelu_tanh(x):
    c = 0.7978845608028654
    h = 0.5 * x
    return h * jnp.tanh(x * ((c * 0.044715) * (x * x) + c)) + h


def _norm_mod(x, g, sc, sh):
    ms = jnp.mean(x * x, axis=-1, keepdims=True)
    gain = g * (1.0 + sc)
    return (x * lax.rsqrt(ms + NORM_EPS)) * gain[:, None, :] + sh[:, None, :]


def _const_spec(shape):
    zeros = (0,) * len(shape)
    return pl.BlockSpec(shape, lambda j: zeros, pipeline_mode=pl.Buffered(1))


def _mod_spec(row_block, col_block):
    return pl.BlockSpec((BATCH, D_MODEL), lambda j: (row_block, col_block),
                        pipeline_mode=pl.Buffered(1))


def _ada_kernel(cond_ref, w_ref, b_ref, o_ref):
    s = _silu(cond_ref[...])
    o_ref[...] = jnp.dot(s, w_ref[...], preferred_element_type=F32,
                         precision=lax.Precision.HIGHEST) + b_ref[...]


def _ada_call(cond, w, b):
    rows = cond.shape[0]
    n_out = w.shape[1]
    tn = D_MODEL
    return pl.pallas_call(
        _ada_kernel,
        grid=(n_out // tn,),
        in_specs=[pl.BlockSpec((rows, D_MODEL), lambda j: (0, 0)),
                  pl.BlockSpec((D_MODEL, tn), lambda j: (0, j)),
                  pl.BlockSpec((1, tn), lambda j: (0, j))],
        out_specs=pl.BlockSpec((rows, tn), lambda j: (0, j)),
        out_shape=jax.ShapeDtypeStruct((rows, n_out), F32),
        compiler_params=pltpu.CompilerParams(dimension_semantics=("arbitrary",)),
        name="ada_mod",
    )(cond, w, b)


def _pre_kernel(with_gate, x_ref, xp_ref, xn_ref, sh_ref, sc_ref, g_ref, w_ref, cw_ref, cb_ref,
                *rest):
    if with_gate:
        xc_ref, sga_ref, hn_s, halo_s, lhs_s, xa_s = rest
        sc = sc_ref[...]
        sh = sh_ref[...]
    else:
        xc_ref, hn_s, halo_s, lhs_s, xa_s = rest
        sc = sc_ref[0:1, :]
        sh = sh_ref[0:1, :]
    j = pl.program_id(0)
    n = pl.num_programs(0)
    g = g_ref[...]

    hn = _norm_mod(x_ref[...], g, sc, sh)
    hp = _norm_mod(xp_ref[...], g, sc, sh)
    hx = _norm_mod(xn_ref[...], g, sc, sh)
    for b in range(BATCH):
        for s in range(N_SLABS):
            cols = slice(s * LANES, (s + 1) * LANES)
            hn_s[s, b * PITCH:b * PITCH + TL, :] = hn[b, :, cols]
            halo_s[s, b * 2 * HALO:b * 2 * HALO + HALO, :] = hp[b, :, cols]
            halo_s[s, b * 2 * HALO + HALO:(b + 1) * 2 * HALO, :] = hx[b, :, cols]

    def build(t2, carry):
        for s in range(N_SLABS):
            v0 = hn_s[s, pl.ds(2 * t2, BATCH, stride=PITCH), :]
            v1 = hn_s[s, pl.ds(2 * t2 + 1, BATCH, stride=PITCH), :]
            row = pl.multiple_of(t2 * 2 * BATCH, 2 * BATCH)
            lhs_s[pl.ds(row, 2 * BATCH), s * LANES:(s + 1) * LANES] = (
                jnp.concatenate([v0, v1], axis=0).astype(BF16))
        return carry

    lax.fori_loop(0, TL // 2, build, 0)
    for s in range(N_SLABS):
        prev = halo_s[s, pl.ds(HALO - 1, BATCH, stride=2 * HALO), :]
        nxt0 = halo_s[s, pl.ds(HALO, BATCH, stride=2 * HALO), :]
        nxt1 = halo_s[s, pl.ds(HALO + 1, BATCH, stride=2 * HALO), :]
        lhs_s[RT:RT + 4 * BATCH, s * LANES:(s + 1) * LANES] = (
            jnp.concatenate([prev, nxt0, nxt1, nxt1], axis=0).astype(BF16))

    xa = jnp.dot(lhs_s[0:RT, :], w_ref[:, 0:D_MODEL], preferred_element_type=F32)
    xh = jnp.dot(lhs_s[RT:RT + 4 * BATCH, :], w_ref[:, 0:D_MODEL], preferred_element_type=F32)
    prev_valid = (j > 0).astype(F32)
    next_valid = (j < n - 1).astype(F32)
    xa_s[0:BATCH, :] = xh[0:BATCH, :] * prev_valid
    xa_s[BATCH:BATCH + RT, :] = xa
    xa_s[BATCH + RT:3 * BATCH + RT, :] = xh[BATCH:3 * BATCH, :] * next_valid

    cw = 0.5 * cw_ref[...]
    acc = xa_s[0:RT, :] * cw[0:1, :]
    for k in range(1, 4):
        acc = acc + xa_s[k * BATCH:k * BATCH + RT, :] * cw[k:k + 1, :]
    xc_ref[...] = acc + 0.5 * cb_ref[...]

    if with_gate:
        ga = jnp.dot(lhs_s[0:RT, :], w_ref[:, D_MODEL:2 * D_MODEL], preferred_element_type=F32)
        sga_ref[...] = _silu(ga).astype(BF16)


def _pre_call(x, mod, norm_g, w, conv_w, conv_b, with_gate):
    _, seq, _ = x.shape
    n = seq // TL
    blocks_per_tile = TL // HALO
    n_halo_blocks = seq // HALO
    mod_rows = 0 if with_gate else 1
    out_shape = [jax.ShapeDtypeStruct((seq * BATCH, D_MODEL), F32)]
    out_specs = [pl.BlockSpec((RT, D_MODEL), lambda j: (j, 0))]
    if with_gate:
        out_shape.append(jax.ShapeDtypeStruct((seq * BATCH, D_MODEL), BF16))
        out_specs.append(pl.BlockSpec((RT, D_MODEL), lambda j: (j, 0)))
    return pl.pallas_call(
        functools.partial(_pre_kernel, with_gate),
        grid=(n,),
        in_specs=[
            pl.BlockSpec((BATCH, TL, D_MODEL), lambda j: (0, j, 0)),
            pl.BlockSpec((BATCH, HALO, D_MODEL),
                         lambda j: (0, jnp.maximum(j * blocks_per_tile - 1, 0), 0)),
            pl.BlockSpec((BATCH, HALO, D_MODEL),
                         lambda j: (0, jnp.minimum((j + 1) * blocks_per_tile, n_halo_blocks - 1), 0)),
            _mod_spec(mod_rows, 0),
            _mod_spec(mod_rows, 1),
            _const_spec((1, D_MODEL)),
            pl.BlockSpec((D_MODEL, (2 if with_gate else 1) * D_MODEL), lambda j: (0, 0),
                         pipeline_mode=pl.Buffered(1)),
            _const_spec((4, D_MODEL)),
            _const_spec((1, D_MODEL)),
        ],
        out_specs=out_specs,
        out_shape=out_shape,
        scratch_shapes=[
            pltpu.VMEM((N_SLABS, BATCH * PITCH, LANES), F32),
            pltpu.VMEM((N_SLABS, BATCH * 2 * HALO, LANES), F32),
            pltpu.VMEM((RT + 4 * BATCH, D_MODEL), BF16),
            pltpu.VMEM((RT + 3 * BATCH, D_MODEL), F32),
        ],
        compiler_params=pltpu.CompilerParams(dimension_semantics=("arbitrary",),
                                             vmem_limit_bytes=VMEM_LIMIT),
        name="lru_pre_gate" if with_gate else "lru_pre",
    )(x, x, x, mod, mod, norm_g, w, conv_w, conv_b)


def _scan_kernel(reverse, out_h, combine, xc_ref, w_ref, ba_ref, bx_ref, lam_ref, h0_ref, *rest):
    rest = list(rest)
    if combine:
        hb_ref = rest.pop(0)
        sga_ref = rest.pop(0)
    hfin_ref = rest.pop(0)
    if out_h:
        h_ref = rest.pop(0)
    if combine:
        y_ref = rest.pop(0)
    carry_s = rest.pop(0)
    a_s = rest.pop(0)
    u_s = rest.pop(0)
    if combine:
        h_ref = rest.pop(0)
        rl_s = rest.pop(0)

    j = pl.program_id(0)

    @pl.when(j == 0)
    def _():
        carry_s[...] = h0_ref[...]

    z = -lam_ref[...]
    k_all = (-0.5 * LRU_C * math.log2(math.e)) * (
        jnp.maximum(z, 0.0) + jnp.log1p(jnp.exp(-jnp.abs(z))))
    for h in range(LRU_HEADS):
        cols = slice(h * HEAD_DIM, (h + 1) * HEAD_DIM)
        xh = xc_ref[:, cols]
        zz = jnp.dot(xh.astype(BF16), w_ref[h], preferred_element_type=F32)
        tr = jnp.tanh(zz[:, 0:HEAD_DIM] + 0.5 * ba_ref[:, cols])
        ti = jnp.tanh(zz[:, HEAD_DIM:2 * HEAD_DIM] + 0.5 * bx_ref[:, cols])
        k = k_all[:, cols]
        a = jnp.exp2(k * tr + k)
        om = 1.0 - a * a
        root = jnp.where(om > 0.0, om * lax.rsqrt(om), 0.0)
        a_s[:, cols] = a
        u_s[:, cols] = ((ti + 1.0) * xh) * root

    def step(kk, hcur):
        t = (TL - 1 - kk) if reverse else kk
        row = pl.multiple_of(t * BATCH, BATCH)
        hnew = a_s[pl.ds(row, BATCH), :] * hcur + u_s[pl.ds(row, BATCH), :]
        if out_h or combine:
            h_ref[pl.ds(row, BATCH), :] = hnew
        return hnew

    hlast = lax.fori_loop(0, TL, step, carry_s[...], unroll=8)
    carry_s[...] = hlast
    hfin_ref[...] = hlast

    if combine:
        for s in range(N_SLABS):
            cols = slice(s * LANES, (s + 1) * LANES)
            rl_s[s] = (h_ref[:, cols] + hb_ref[:, cols]) * sga_ref[:, cols].astype(F32)
        for b in range(BATCH):
            for s in range(N_SLABS):
                y_ref[b, :, s * LANES:(s + 1) * LANES] = (
                    rl_s[s, pl.ds(b, TL, stride=BATCH), :].astype(BF16))


def _scan_call(xc, w, ba, bx, lam, h0, hb=None, sga=None, *, reverse, out_h, combine):
    assert not (out_h and combine)
    rows = xc.shape[0]
    seq = rows // BATCH
    n = seq // TL
    tile = (lambda j: (n - 1 - j, 0)) if reverse else (lambda j: (j, 0))
    in_specs = [
        pl.BlockSpec((RT, D_MODEL), tile),
        _const_spec((LRU_HEADS, HEAD_DIM, 2 * HEAD_DIM)),
        _const_spec((1, D_MODEL)),
        _const_spec((1, D_MODEL)),
        _const_spec((1, D_MODEL)),
        _const_spec((BATCH, D_MODEL)),
    ]
    args = [xc, w, ba, bx, lam, h0]
    if combine:
        in_specs += [pl.BlockSpec((RT, D_MODEL), tile), pl.BlockSpec((RT, D_MODEL), tile)]
        args += [hb, sga]
    out_shape = [jax.ShapeDtypeStruct((BATCH, D_MODEL), F32)]
    out_specs = [pl.BlockSpec((BATCH, D_MODEL), lambda j: (0, 0))]
    if out_h:
        out_shape.append(jax.ShapeDtypeStruct((rows, D_MODEL), F32))
        out_specs.append(pl.BlockSpec((RT, D_MODEL), tile))
    if combine:
        out_shape.append(jax.ShapeDtypeStruct((BATCH, seq, D_MODEL), BF16))
        out_specs.append(pl.BlockSpec((BATCH, TL, D_MODEL),
                                      (lambda j: (0, n - 1 - j, 0)) if reverse else (lambda j: (0, j, 0))))
    scratch = [pltpu.VMEM((BATCH, D_MODEL), F32),
               pltpu.VMEM((RT, D_MODEL), F32),
               pltpu.VMEM((RT, D_MODEL), F32)]
    if combine:
        scratch.append(pltpu.VMEM((RT, D_MODEL), F32))
        scratch.append(pltpu.VMEM((N_SLABS, RT, LANES), F32))
    return pl.pallas_call(
        functools.partial(_scan_kernel, reverse, out_h, combine),
        grid=(n,),
        in_specs=in_specs,
        out_specs=out_specs,
        out_shape=out_shape,
        scratch_shapes=scratch,
        compiler_params=pltpu.CompilerParams(dimension_semantics=("arbitrary",),
                                             vmem_limit_bytes=VMEM_LIMIT),
        name="lru_scan_" + ("bwd" if reverse else "fwd") + ("_combine" if combine else ""),
    )(*args)


def _final_kernel(x_ref, sh_ref, sc_ref, gx_ref, g_ref, w_ref, lng_ref, lnb_ref, ws_ref, bs_ref,
                  yl_ref, wo_ref, fg_ref, o_ref, vn_s, y_s):
    x = x_ref[...]
    hn = _norm_mod(x, g_ref[...], sc_ref[...], sh_ref[...])
    lhs = hn.reshape(RT, D_MODEL).astype(BF16)

    v = _gelu_tanh(jnp.dot(lhs, w_ref[:, D_MODEL:2 * D_MODEL], preferred_element_type=F32))
    mu = jnp.mean(v, axis=-1, keepdims=True)
    vc = v - mu
    var = jnp.mean(vc * vc, axis=-1, keepdims=True)
    vn_s[...] = (vc * lax.rsqrt(var + LN_EPS) * lng_ref[...] + lnb_ref[...]).astype(BF16)

    u = _gelu_tanh(jnp.dot(lhs, w_ref[:, 0:D_MODEL], preferred_element_type=F32))
    gb = _silu(jnp.dot(lhs, w_ref[:, 2 * D_MODEL:3 * D_MODEL], preferred_element_type=F32))
    ug = u * gb

    y_s[:, 0:D_MODEL] = yl_ref[...].reshape(RT, D_MODEL)
    for gidx in range(SGU_GROUPS):
        cols = slice(gidx * LANES, (gidx + 1) * LANES)
        rhs = jnp.concatenate([vn_s[b * TL:(b + 1) * TL, cols] for b in range(BATCH)], axis=1)
        mixed = jnp.dot(ws_ref[gidx], rhs, preferred_element_type=F32)
        for b in range(BATCH):
            rows = slice(b * TL, (b + 1) * TL)
            m_b = mixed[:, b * LANES:(b + 1) * LANES] + bs_ref[gidx]
            y_s[rows, D_MODEL + gidx * LANES:D_MODEL + (gidx + 1) * LANES] = (
                ug[rows, cols] * m_b).astype(BF16)

    mix = jnp.dot(y_s[...], wo_ref[...], preferred_element_type=F32)
    xnew = x + gx_ref[...][:, None, :] * mix.reshape(BATCH, TL, D_MODEL)
    ms = jnp.mean(xnew * xnew, axis=-1, keepdims=True)
    o_ref[...] = xnew * lax.rsqrt(ms + NORM_EPS) * fg_ref[...]


def _final_call(x, mod, norm_g, w_uvg, ln_g, ln_b, w_s, b_s, ylg, w_out, final_g):
    _, seq, _ = x.shape
    n = seq // TL
    tile = pl.BlockSpec((BATCH, TL, D_MODEL), lambda j: (0, j, 0))
    return pl.pallas_call(
        _final_kernel,
        grid=(n,),
        in_specs=[
            tile,
            _mod_spec(0, 0),
            _mod_spec(0, 1),
            _mod_spec(0, 2),
            _const_spec((1, D_MODEL)),
            _const_spec(w_uvg.shape),
            _const_spec((1, D_MODEL)),
            _const_spec((1, D_MODEL)),
            _const_spec((SGU_GROUPS, CHUNK, CHUNK)),
            _const_spec((SGU_GROUPS, CHUNK, LANES)),
            tile,
            _const_spec(w_out.shape),
            _const_spec((1, D_MODEL)),
        ],
        out_specs=tile,
        out_shape=jax.ShapeDtypeStruct(x.shape, F32),
        scratch_shapes=[pltpu.VMEM((RT, D_MODEL), BF16),
                        pltpu.VMEM((RT, 2 * D_MODEL), BF16)],
        compiler_params=pltpu.CompilerParams(dimension_semantics=("arbitrary",),
                                             vmem_limit_bytes=VMEM_LIMIT),
        name="sgu_out_final",
    )(x, mod, mod, mod, norm_g, w_uvg, ln_g, ln_b, w_s, b_s, ylg, w_out, final_g)


def kernel(x, c, ctx, c_ctx, ada_w, ada_b, norm_g, w_in, conv_w, conv_b, lru_wa, lru_ba,
           lru_wx, lru_bx, lru_lambda, sgu_ln_g, sgu_ln_b, sgu_w, sgu_b, w_out, final_g):
    assert x.shape == (BATCH, x.shape[1], D_MODEL) and x.shape[1] % TL == 0
    assert ctx.shape == (BATCH, ctx.shape[1], D_MODEL) and ctx.shape[1] % TL == 0
    assert ada_w.shape[0] == 1, "single-layer block"
    layer = 0
    W = D_MODEL

    cond = jnp.concatenate([c, c_ctx[None, :], jnp.zeros((BATCH - 1, D_MODEL), F32)], axis=0)
    mod = _ada_call(cond, ada_w[layer], ada_b[layer][None, :])

    ng = norm_g[layer][None, :]
    w_lru = w_in[layer][:, :2 * W].astype(BF16)
    w_uvg = w_in[layer][:, 2 * W:].astype(BF16)
    cw = conv_w[layer]
    cb = conv_b[layer][None, :]
    w_gate = jnp.concatenate([lru_wa[layer], lru_wx[layer]], axis=-1).astype(BF16)
    ba = lru_ba[layer].reshape(2, 1, W)
    bx = lru_bx[layer].reshape(2, 1, W)
    lam = lru_lambda[layer].reshape(2, 1, W)
    zeros = jnp.zeros((BATCH, W), F32)

    def scan(xc, d, h0, **kw):
        return _scan_call(xc, w_gate[d], ba[d], bx[d], lam[d], h0, reverse=(d == 1), **kw)

    (xc_c,) = _pre_call(ctx, mod, ng, w_lru, cw, cb, with_gate=False)
    (hb_c,) = scan(xc_c, 1, zeros, out_h=False, combine=False)
    (hf_c,) = scan(xc_c, 0, zeros, out_h=False, combine=False)

    xc, sga = _pre_call(x, mod, ng, w_lru, cw, cb, with_gate=True)
    _, hb = scan(xc, 1, hb_c, out_h=True, combine=False)
    _, ylg = scan(xc, 0, hf_c, hb=hb, sga=sga, out_h=False, combine=True)

    bs = jnp.broadcast_to(sgu_b[layer][:, :, None], (SGU_GROUPS, CHUNK, LANES))
    return _final_call(x, mod, ng, w_uvg, sgu_ln_g[layer][None, :], sgu_ln_b[layer][None, :],
                       sgu_w[layer].astype(BF16), bs, ylg, w_out[layer].astype(BF16),
                       final_g[None, :])
```

```python
import functools
import math

import jax
import jax.numpy as jnp
from jax import lax
from jax.experimental import pallas as pl
from jax.experimental.pallas import tpu as pltpu

F32 = jnp.float32
BF16 = jnp.bfloat16

D_MODEL = 1024
BATCH = 8
LRU_HEADS = 8
HEAD_DIM = 128
SGU_GROUPS = 8
CHUNK = 128
LRU_C = 8.0
NORM_EPS = 1e-6
LN_EPS = 1e-5

LANES = 128
SUBLANES = 8
N_SLABS = D_MODEL // LANES
TL = CHUNK
RT = TL * BATCH
TB = 32
RB = TB * BATCH
N_RB = TL // TB
PITCH = TL + SUBLANES
HALO = SUBLANES
VMEM_LIMIT = 56 * 1024 * 1024


def _silu(x):
    h = 0.5 * x
    return h * jnp.tanh(h) + h


def _gelu_tanh(x):
    c = 0.7978845608028654
    h = 0.5 * x
    return h * jnp.tanh(x * ((c * 0.044715) * (x * x) + c)) + h


def _norm_gain(x, gain, sh):
    ms = jnp.mean(x * x, axis=-1, keepdims=True)
    return (x * lax.rsqrt(ms + NORM_EPS)) * gain[:, None, :] + sh[:, None, :]


def _lru_coef(lam):
    z = -lam
    return (-0.5 * LRU_C * math.log2(math.e)) * (
        jnp.maximum(z, 0.0) + jnp.log1p(jnp.exp(-jnp.abs(z))))


def _lru_gates(xh, w_h, hba, hbx, k):
    zz = jnp.dot(xh.astype(BF16), w_h, preferred_element_type=F32)
    tr = jnp.tanh(zz[:, 0:HEAD_DIM] + hba)
    ti = jnp.tanh(zz[:, HEAD_DIM:2 * HEAD_DIM] + hbx)
    a = jnp.exp2(k * tr + k)
    om = 1.0 - a * a
    root = jnp.where(om > 0.0, om * lax.rsqrt(om), 0.0)
    return a, ((ti + 1.0) * xh) * root


def _const_spec(shape):
    zeros = (0,) * len(shape)
    return pl.BlockSpec(shape, lambda j: zeros, pipeline_mode=pl.Buffered(1))


def _mod_spec(row_block, col_block):
    return pl.BlockSpec((BATCH, D_MODEL), lambda j: (row_block, col_block),
                        pipeline_mode=pl.Buffered(1))


def _ada_kernel(cond_ref, w_ref, b_ref, o_ref):
    s = _silu(cond_ref[...])
    o_ref[...] = jnp.dot(s, w_ref[...], preferred_element_type=F32,
                         precision=lax.Precision.HIGHEST) + b_ref[...]


def _ada_call(cond, w, b):
    rows = cond.shape[0]
    n_out = w.shape[1]
    tn = D_MODEL
    return pl.pallas_call(
        _ada_kernel,
        grid=(n_out // tn,),
        in_specs=[pl.BlockSpec((rows, D_MODEL), lambda j: (0, 0)),
                  pl.BlockSpec((D_MODEL, tn), lambda j: (0, j)),
                  pl.BlockSpec((1, tn), lambda j: (0, j))],
        out_specs=pl.BlockSpec((rows, tn), lambda j: (0, j)),
        out_shape=jax.ShapeDtypeStruct((rows, n_out), F32),
        compiler_params=pltpu.CompilerParams(dimension_semantics=("arbitrary",)),
        name="ada_mod",
    )(cond, w, b)


def _pre_bwd_kernel(latent, x_ref, xp_ref, xn_ref, sh_ref, sc_ref, g_ref, w_ref, cw_ref, cb_ref,
                    wg_ref, ba_ref, bx_ref, lam_ref, h0_ref, *rest):
    if latent:
        xc_ref, hfin_ref, hb_ref, sga_ref, hn_s, halo_s, lhs_s, xa_s, carry_s = rest
        sc = sc_ref[...]
        sh = sh_ref[...]
    else:
        xc_ref, hfin_ref, hn_s, halo_s, lhs_s, xa_s, carry_s = rest
        sc = sc_ref[0:1, :]
        sh = sh_ref[0:1, :]
    j = pl.program_id(0)
    n = pl.num_programs(0)

    @pl.when(j == 0)
    def _():
        carry_s[...] = h0_ref[...]

    gain = g_ref[...] * (1.0 + sc)

    hp = _norm_gain(xp_ref[...], gain, sh)
    hx = _norm_gain(xn_ref[...], gain, sh)
    for b in range(BATCH):
        for s in range(N_SLABS):
            cols = slice(s * LANES, (s + 1) * LANES)
            halo_s[s, b * 2 * HALO:b * 2 * HALO + HALO, :] = hp[b, :, cols]
            halo_s[s, b * 2 * HALO + HALO:(b + 1) * 2 * HALO, :] = hx[b, :, cols]
    for s in range(N_SLABS):
        prev = halo_s[s, pl.ds(HALO - 1, BATCH, stride=2 * HALO), :]
        nxt0 = halo_s[s, pl.ds(HALO, BATCH, stride=2 * HALO), :]
        nxt1 = halo_s[s, pl.ds(HALO + 1, BATCH, stride=2 * HALO), :]
        lhs_s[RT:RT + 4 * BATCH, s * LANES:(s + 1) * LANES] = (
            jnp.concatenate([prev, nxt0, nxt1, nxt1], axis=0).astype(BF16))
    prev_valid = (j < n - 1).astype(F32)
    next_valid = (j > 0).astype(F32)

    cw = 0.5 * cw_ref[...]
    cb = 0.5 * cb_ref[...]
    k_all = _lru_coef(lam_ref[...])
    hba = 0.5 * ba_ref[...]
    hbx = 0.5 * bx_ref[...]
    hcur = [carry_s[:, h * HEAD_DIM:(h + 1) * HEAD_DIM] for h in range(LRU_HEADS)]

    def project(rb):
        t0 = rb * TB
        hn = _norm_gain(x_ref[:, t0:t0 + TB, :], gain, sh)
        for b in range(BATCH):
            for s in range(N_SLABS):
                hn_s[s, b * PITCH + t0:b * PITCH + t0 + TB, :] = hn[b, :, s * LANES:(s + 1) * LANES]
        for t2 in range(TB // 2):
            for s in range(N_SLABS):
                v0 = hn_s[s, pl.ds(t0 + 2 * t2, BATCH, stride=PITCH), :]
                v1 = hn_s[s, pl.ds(t0 + 2 * t2 + 1, BATCH, stride=PITCH), :]
                r = rb * RB + t2 * 2 * BATCH
                lhs_s[r:r + 2 * BATCH, s * LANES:(s + 1) * LANES] = (
                    jnp.concatenate([v0, v1], axis=0).astype(BF16))
        rows = slice(rb * RB, (rb + 1) * RB)
        if rb == N_RB - 1:
            xa = jnp.dot(lhs_s[rb * RB:RT + 4 * BATCH, :], w_ref[:, 0:D_MODEL],
                         preferred_element_type=F32)
            xa_s[0:BATCH, :] = xa[RB:RB + BATCH, :] * prev_valid
            xa_s[BATCH + RT:3 * BATCH + RT, :] = xa[RB + BATCH:RB + 3 * BATCH, :] * next_valid
            xa_s[BATCH + rb * RB:BATCH + RT, :] = xa[0:RB, :]
        else:
            xa_s[BATCH + rb * RB:BATCH + (rb + 1) * RB, :] = jnp.dot(
                lhs_s[rows, :], w_ref[:, 0:D_MODEL], preferred_element_type=F32)
        if latent:
            ga = jnp.dot(lhs_s[rows, :], w_ref[:, D_MODEL:2 * D_MODEL], preferred_element_type=F32)
            sga_ref[rows, :] = _silu(ga).astype(BF16)

    def recur(rb):
        r0 = rb * RB
        acc = xa_s[r0:r0 + RB, :] * cw[0:1, :]
        for k in range(1, 4):
            acc = acc + xa_s[r0 + k * BATCH:r0 + k * BATCH + RB, :] * cw[k:k + 1, :]
        xc = acc + cb
        xc_ref[r0:r0 + RB, :] = xc
        for h in range(LRU_HEADS):
            cols = slice(h * HEAD_DIM, (h + 1) * HEAD_DIM)
            a, u = _lru_gates(xc[:, cols], wg_ref[h], hba[:, cols], hbx[:, cols], k_all[:, cols])
            for t in reversed(range(TB)):
                step = slice(t * BATCH, (t + 1) * BATCH)
                hcur[h] = a[step, :] * hcur[h] + u[step, :]
                if latent:
                    hb_ref[r0 + t * BATCH:r0 + (t + 1) * BATCH, cols] = hcur[h]

    for rb in reversed(range(N_RB)):
        project(rb)
        if rb + 1 < N_RB:
            recur(rb + 1)
    recur(0)

    for h in range(LRU_HEADS):
        cols = slice(h * HEAD_DIM, (h + 1) * HEAD_DIM)
        carry_s[:, cols] = hcur[h]
        hfin_ref[:, cols] = hcur[h]


def _pre_bwd_call(x, mod, norm_g, w, conv_w, conv_b, w_gate, ba, bx, lam, h0, latent):
    _, seq, _ = x.shape
    n = seq // TL
    blocks_per_tile = TL // HALO
    n_halo_blocks = seq // HALO
    mod_rows = 0 if latent else 1
    tile = lambda j: (n - 1 - j, 0)
    out_shape = [jax.ShapeDtypeStruct((seq * BATCH, D_MODEL), F32),
                 jax.ShapeDtypeStruct((BATCH, D_MODEL), F32)]
    out_specs = [pl.BlockSpec((RT, D_MODEL), tile),
                 pl.BlockSpec((BATCH, D_MODEL), lambda j: (0, 0))]
    if latent:
        out_shape += [jax.ShapeDtypeStruct((seq * BATCH, D_MODEL), F32),
                      jax.ShapeDtypeStruct((seq * BATCH, D_MODEL), BF16)]
        out_specs += [pl.BlockSpec((RT, D_MODEL), tile), pl.BlockSpec((RT, D_MODEL), tile)]
    return pl.pallas_call(
        functools.partial(_pre_bwd_kernel, latent),
        grid=(n,),
        in_specs=[
            pl.BlockSpec((BATCH, TL, D_MODEL), lambda j: (0, n - 1 - j, 0)),
            pl.BlockSpec((BATCH, HALO, D_MODEL),
                         lambda j: (0, jnp.maximum((n - 1 - j) * blocks_per_tile - 1, 0), 0)),
            pl.BlockSpec((BATCH, HALO, D_MODEL),
                         lambda j: (0, jnp.minimum((n - j) * blocks_per_tile, n_halo_blocks - 1), 0)),
            _mod_spec(mod_rows, 0),
            _mod_spec(mod_rows, 1),
            _const_spec((1, D_MODEL)),
            pl.BlockSpec((D_MODEL, (2 if latent else 1) * D_MODEL), lambda j: (0, 0),
                         pipeline_mode=pl.Buffered(1)),
            _const_spec((4, D_MODEL)),
            _const_spec((1, D_MODEL)),
            _const_spec((LRU_HEADS, HEAD_DIM, 2 * HEAD_DIM)),
            _const_spec((1, D_MODEL)),
            _const_spec((1, D_MODEL)),
            _const_spec((1, D_MODEL)),
            _const_spec((BATCH, D_MODEL)),
        ],
        out_specs=out_specs,
        out_shape=out_shape,
        scratch_shapes=[
            pltpu.VMEM((N_SLABS, BATCH * PITCH, LANES), F32),
            pltpu.VMEM((N_SLABS, BATCH * 2 * HALO, LANES), F32),
            pltpu.VMEM((RT + 4 * BATCH, D_MODEL), BF16),
            pltpu.VMEM((RT + 3 * BATCH, D_MODEL), F32),
            pltpu.VMEM((BATCH, D_MODEL), F32),
        ],
        compiler_params=pltpu.CompilerParams(dimension_semantics=("arbitrary",),
                                             vmem_limit_bytes=VMEM_LIMIT),
        name="lru_pre_bwd" if latent else "ctx_pre_bwd",
    )(x, x, x, mod, mod, norm_g, w, conv_w, conv_b, w_gate, ba, bx, lam, h0)


def _scan_kernel(combine, xc_ref, w_ref, ba_ref, bx_ref, lam_ref, h0_ref, *rest):
    if combine:
        hb_ref, sga_ref, hfin_ref, y_ref, carry_s, a_s, u_s, h_s, rl_s = rest
    else:
        hfin_ref, carry_s, a_s, u_s = rest

    j = pl.program_id(0)

    @pl.when(j == 0)
    def _():
        carry_s[...] = h0_ref[...]

    k_all = _lru_coef(lam_ref[...])
    hba = 0.5 * ba_ref[...]
    hbx = 0.5 * bx_ref[...]
    for h in range(LRU_HEADS):
        cols = slice(h * HEAD_DIM, (h + 1) * HEAD_DIM)
        a, u = _lru_gates(xc_ref[:, cols], w_ref[h], hba[:, cols], hbx[:, cols], k_all[:, cols])
        a_s[:, cols] = a
        u_s[:, cols] = u

    def step(kk, hcur):
        row = pl.multiple_of(kk * BATCH, BATCH)
        hnew = a_s[pl.ds(row, BATCH), :] * hcur + u_s[pl.ds(row, BATCH), :]
        if combine:
            h_s[pl.ds(row, BATCH), :] = hnew
        return hnew

    hlast = lax.fori_loop(0, TL, step, carry_s[...], unroll=8)
    carry_s[...] = hlast
    hfin_ref[...] = hlast

    if combine:
        for s in range(N_SLABS):
            cols = slice(s * LANES, (s + 1) * LANES)
            rl_s[s] = (h_s[:, cols] + hb_ref[:, cols]) * sga_ref[:, cols].astype(F32)
        for b in range(BATCH):
            for s in range(N_SLABS):
                y_ref[b, :, s * LANES:(s + 1) * LANES] = (
                    rl_s[s, pl.ds(b, TL, stride=BATCH), :].astype(BF16))


def _scan_call(xc, w, ba, bx, lam, h0, hb=None, sga=None, *, combine):
    rows = xc.shape[0]
    seq = rows // BATCH
    n = seq // TL
    tile = lambda j: (j, 0)
    in_specs = [
        pl.BlockSpec((RT, D_MODEL), tile),
        _const_spec((LRU_HEADS, HEAD_DIM, 2 * HEAD_DIM)),
        _const_spec((1, D_MODEL)),
        _const_spec((1, D_MODEL)),
        _const_spec((1, D_MODEL)),
        _const_spec((BATCH, D_MODEL)),
    ]
    args = [xc, w, ba, bx, lam, h0]
    if combine:
        in_specs += [pl.BlockSpec((RT, D_MODEL), tile), pl.BlockSpec((RT, D_MODEL), tile)]
        args += [hb, sga]
    out_shape = [jax.ShapeDtypeStruct((BATCH, D_MODEL), F32)]
    out_specs = [pl.BlockSpec((BATCH, D_MODEL), lambda j: (0, 0))]
    if combine:
        out_shape.append(jax.ShapeDtypeStruct((BATCH, seq, D_MODEL), BF16))
        out_specs.append(pl.BlockSpec((BATCH, TL, D_MODEL), lambda j: (0, j, 0)))
    scratch = [pltpu.VMEM((BATCH, D_MODEL), F32),
               pltpu.VMEM((RT, D_MODEL), F32),
               pltpu.VMEM((RT, D_MODEL), F32)]
    if combine:
        scratch.append(pltpu.VMEM((RT, D_MODEL), F32))
        scratch.append(pltpu.VMEM((N_SLABS, RT, LANES), F32))
    return pl.pallas_call(
        functools.partial(_scan_kernel, combine),
        grid=(n,),
        in_specs=in_specs,
        out_specs=out_specs,
        out_shape=out_shape,
        scratch_shapes=scratch,
        compiler_params=pltpu.CompilerParams(dimension_semantics=("arbitrary",),
                                             vmem_limit_bytes=VMEM_LIMIT),
        name="lru_scan_fwd" + ("_combine" if combine else ""),
    )(*args)


def _final_kernel(x_ref, sh_ref, sc_ref, gx_ref, g_ref, w_ref, lng_ref, lnb_ref, ws_ref, bs_ref,
                  yl_ref, wo_ref, fg_ref, o_ref, vn_s, y_s):
    x = x_ref[...]
    hn = _norm_gain(x, g_ref[...] * (1.0 + sc_ref[...]), sh_ref[...])
    lhs = hn.reshape(RT, D_MODEL).astype(BF16)

    v = _gelu_tanh(jnp.dot(lhs, w_ref[:, D_MODEL:2 * D_MODEL], preferred_element_type=F32))
    mu = jnp.mean(v, axis=-1, keepdims=True)
    vc = v - mu
    var = jnp.mean(vc * vc, axis=-1, keepdims=True)
    vn_s[...] = (vc * lax.rsqrt(var + LN_EPS) * lng_ref[...] + lnb_ref[...]).astype(BF16)

    u = _gelu_tanh(jnp.dot(lhs, w_ref[:, 0:D_MODEL], preferred_element_type=F32))
    gb = _silu(jnp.dot(lhs, w_ref[:, 2 * D_MODEL:3 * D_MODEL], preferred_element_type=F32))
    ug = u * gb

    y_s[:, 0:D_MODEL] = yl_ref[...].reshape(RT, D_MODEL)
    for gidx in range(SGU_GROUPS):
        cols = slice(gidx * LANES, (gidx + 1) * LANES)
        rhs = jnp.concatenate([vn_s[b * TL:(b + 1) * TL, cols] for b in range(BATCH)], axis=1)
        mixed = jnp.dot(ws_ref[gidx], rhs, preferred_element_type=F32)
        for b in range(BATCH):
            rows = slice(b * TL, (b + 1) * TL)
            m_b = mixed[:, b * LANES:(b + 1) * LANES] + bs_ref[gidx]
            y_s[rows, D_MODEL + gidx * LANES:D_MODEL + (gidx + 1) * LANES] = (
                ug[rows, cols] * m_b).astype(BF16)

    mix = jnp.dot(y_s[...], wo_ref[...], preferred_element_type=F32)
    xnew = x + gx_ref[...][:, None, :] * mix.reshape(BATCH, TL, D_MODEL)
    ms = jnp.mean(xnew * xnew, axis=-1, keepdims=True)
    o_ref[...] = xnew * lax.rsqrt(ms + NORM_EPS) * fg_ref[...]


def _final_call(x, mod, norm_g, w_uvg, ln_g, ln_b, w_s, b_s, ylg, w_out, final_g):
    _, seq, _ = x.shape
    n = seq // TL
    tile = pl.BlockSpec((BATCH, TL, D_MODEL), lambda j: (0, j, 0))
    return pl.pallas_call(
        _final_kernel,
        grid=(n,),
        in_specs=[
            tile,
            _mod_spec(0, 0),
            _mod_spec(0, 1),
            _mod_spec(0, 2),
            _const_spec((1, D_MODEL)),
            _const_spec(w_uvg.shape),
            _const_spec((1, D_MODEL)),
            _const_spec((1, D_MODEL)),
            _const_spec((SGU_GROUPS, CHUNK, CHUNK)),
            _const_spec((SGU_GROUPS, CHUNK, LANES)),
            tile,
            _const_spec(w_out.shape),
            _const_spec((1, D_MODEL)),
        ],
        out_specs=tile,
        out_shape=jax.ShapeDtypeStruct(x.shape, F32),
        scratch_shapes=[pltpu.VMEM((RT, D_MODEL), BF16),
                        pltpu.VMEM((RT, 2 * D_MODEL), BF16)],
        compiler_params=pltpu.CompilerParams(dimension_semantics=("arbitrary",),
                                             vmem_limit_bytes=VMEM_LIMIT),
        name="sgu_out_final",
    )(x, mod, mod, mod, norm_g, w_uvg, ln_g, ln_b, w_s, b_s, ylg, w_out, final_g)


def kernel(x, c, ctx, c_ctx, ada_w, ada_b, norm_g, w_in, conv_w, conv_b, lru_wa, lru_ba,
           lru_wx, lru_bx, lru_lambda, sgu_ln_g, sgu_ln_b, sgu_w, sgu_b, w_out, final_g):
    assert x.shape == (BATCH, x.shape[1], D_MODEL) and x.shape[1] % TL == 0
    assert ctx.shape == (BATCH, ctx.shape[1], D_MODEL) and ctx.shape[1] % TL == 0
    assert ada_w.shape[0] == 1, "single-layer block"
    layer = 0
    W = D_MODEL

    cond = jnp.concatenate([c, c_ctx[None, :], jnp.zeros((BATCH - 1, D_MODEL), F32)], axis=0)
    mod = _ada_call(cond, ada_w[layer], ada_b[layer][None, :])

    ng = norm_g[layer][None, :]
    w_lru = w_in[layer][:, :2 * W].astype(BF16)
    w_uvg = w_in[layer][:, 2 * W:].astype(BF16)
    cw = conv_w[layer]
    cb = conv_b[layer][None, :]
    w_gate = jnp.concatenate([lru_wa[layer], lru_wx[layer]], axis=-1).astype(BF16)
    ba = lru_ba[layer].reshape(2, 1, W)
    bx = lru_bx[layer].reshape(2, 1, W)
    lam = lru_lambda[layer].reshape(2, 1, W)
    zeros = jnp.zeros((BATCH, W), F32)
    FWD, BWD = 0, 1

    def pre_bwd(inp, h0, latent):
        return _pre_bwd_call(inp, mod, ng, w_lru, cw, cb, w_gate[BWD], ba[BWD], bx[BWD], lam[BWD],
                             h0, latent)

    def scan_fwd(xc, h0, **kw):
        return _scan_call(xc, w_gate[FWD], ba[FWD], bx[FWD], lam[FWD], h0, **kw)

    xc_c, hb_c = pre_bwd(ctx, zeros, latent=False)
    (hf_c,) = scan_fwd(xc_c, zeros, combine=False)

    xc, _, hb, sga = pre_bwd(x, hb_c, latent=True)
    _, ylg = scan_fwd(xc, hf_c, hb=hb, sga=sga, combine=True)

    bs = jnp.broadcast_to(sgu_b[layer][:, :, None], (SGU_GROUPS, CHUNK, LANES))
    return _final_call(x, mod, ng, w_uvg, sgu_ln_g[layer][None, :], sgu_ln_b[layer][None, :],
                       sgu_w[layer].astype(BF16), bs, ylg, w_out[layer].astype(BF16),
                       final_g[None, :])
```

```python
import functools
import math

import jax
import jax.numpy as jnp
from jax import lax
from jax.experimental import pallas as pl
from jax.experimental.pallas import tpu as pltpu

F32 = jnp.float32
BF16 = jnp.bfloat16

D_MODEL = 1024
BATCH = 8
LRU_HEADS = 8
HEAD_DIM = 128
SGU_GROUPS = 8
CHUNK = 128
LRU_C = 8.0
NORM_EPS = 1e-6
LN_EPS = 1e-5

LANES = 128
N_SLABS = D_MODEL // LANES
TL = CHUNK
RT = TL * BATCH
TB = 32
RB = TB * BATCH
N_RB = TL // TB
HALO_T = 4
HALO_ROWS = HALO_T * BATCH
VMEM_LIMIT = 56 * 1024 * 1024


def _silu(x):
    h = 0.5 * x
    return h * jnp.tanh(h) + h


def _gelu_tanh(x):
    c = 0.7978845608028654
    h = 0.5 * x
    return h * jnp.tanh(x * ((c * 0.044715) * (x * x) + c)) + h


def _rms_scale(x):
    return x * lax.rsqrt(jnp.mean(x * x, axis=-1, keepdims=True) + NORM_EPS)


def _lru_coef(lam):
    z = -lam
    return (-0.5 * LRU_C * math.log2(math.e)) * (
        jnp.maximum(z, 0.0) + jnp.log1p(jnp.exp(-jnp.abs(z))))


def _lru_gates(xb, xh, w_h, hba, hbx, k):
    zz = jnp.dot(xb, w_h, preferred_element_type=F32)
    tr = jnp.tanh(zz[:, 0:HEAD_DIM] + hba)
    ti = jnp.tanh(zz[:, HEAD_DIM:2 * HEAD_DIM] + hbx)
    a = jnp.exp2(k * tr + k)
    om = 1.0 - a * a
    root = jnp.where(om > 0.0, om * lax.rsqrt(om), 0.0)
    return a, ((ti + 1.0) * xh) * root


def _const_spec(shape):
    zeros = (0,) * len(shape)
    return pl.BlockSpec(shape, lambda j: zeros, pipeline_mode=pl.Buffered(1))


def _mod_spec(row_block, col_block):
    return pl.BlockSpec((BATCH, D_MODEL), lambda j: (row_block, col_block),
                        pipeline_mode=pl.Buffered(1))


def _ada_kernel(cond_ref, w_ref, b_ref, o_ref):
    s = _silu(cond_ref[...])
    o_ref[...] = jnp.dot(s, w_ref[...], preferred_element_type=F32,
                         precision=lax.Precision.HIGHEST) + b_ref[...]


def _ada_call(cond, w, b):
    rows = cond.shape[0]
    n_out = w.shape[1]
    tn = D_MODEL
    return pl.pallas_call(
        _ada_kernel,
        grid=(n_out // tn,),
        in_specs=[pl.BlockSpec((rows, D_MODEL), lambda j: (0, 0)),
                  pl.BlockSpec((D_MODEL, tn), lambda j: (0, j)),
                  pl.BlockSpec((1, tn), lambda j: (0, j))],
        out_specs=pl.BlockSpec((rows, tn), lambda j: (0, j)),
        out_shape=jax.ShapeDtypeStruct((rows, n_out), F32),
        compiler_params=pltpu.CompilerParams(dimension_semantics=("arbitrary",)),
        name="ada_mod",
    )(cond, w, b)


def _pre_bwd_kernel(latent, x_hbm, sh_ref, sc_ref, g_ref, w_ref, cw_ref, cb_ref,
                    wg_ref, ba_ref, bx_ref, lam_ref, h0_ref, *rest):
    if latent:
        xc_ref, hfin_ref, hb_ref, sga_ref, xbuf, sem, lhs_s, xa_s, carry_s = rest
        sc = sc_ref[...]
        sh = sh_ref[...]
    else:
        xc_ref, hfin_ref, xbuf, sem, lhs_s, xa_s, carry_s = rest
        sc = sc_ref[0:1, :]
        sh = sh_ref[0:1, :]
    j = pl.program_id(0)
    n = pl.num_programs(0)
    seq = n * TL
    slot = j % 2

    def tile_copies(tile, buf_slot):
        t0 = tile * TL
        copies = []
        for b in range(BATCH):
            for src_t, dst_t, cnt in ((t0, 0, TL),
                                      (jnp.maximum(t0 - 1, 0), TL, 1),
                                      (jnp.minimum(t0 + TL, seq - 2), TL + 1, 2)):
                copies.append(pltpu.make_async_copy(
                    x_hbm.at[b, pl.ds(src_t, cnt), :],
                    xbuf.at[buf_slot, pl.ds(dst_t, cnt), b, :],
                    sem.at[buf_slot]))
        return copies

    @pl.when(j == 0)
    def _():
        carry_s[...] = h0_ref[...]
        xbuf[:, TL + HALO_T - 1:TL + HALO_T, :, :] = jnp.zeros((2, 1, BATCH, D_MODEL), F32)
        for cp in tile_copies(n - 1, 0):
            cp.start()

    for cp in tile_copies(n - 1 - j, slot):
        cp.wait()

    @pl.when(j + 1 < n)
    def _():
        for cp in tile_copies(n - 2 - j, 1 - slot):
            cp.start()

    gain = g_ref[...] * (1.0 + sc)

    def norm_rows(xv):
        hn = _rms_scale(xv) * gain[None, :, :] + sh[None, :, :]
        return hn.reshape(xv.shape[0] * BATCH, D_MODEL).astype(BF16)

    lhs_s[RT:RT + HALO_ROWS, :] = norm_rows(xbuf[slot, TL:TL + HALO_T])
    prev_valid = (j < n - 1).astype(F32)
    next_valid = (j > 0).astype(F32)

    cw = 0.5 * cw_ref[...]
    cb = 0.5 * cb_ref[...]
    k_all = _lru_coef(lam_ref[...])
    hba = 0.5 * ba_ref[...]
    hbx = 0.5 * bx_ref[...]
    hcur = [carry_s[:, h * HEAD_DIM:(h + 1) * HEAD_DIM] for h in range(LRU_HEADS)]

    def project(rb):
        t0 = rb * TB
        rows = slice(rb * RB, (rb + 1) * RB)
        lhs_s[rows, :] = norm_rows(xbuf[slot, t0:t0 + TB])
        if rb == N_RB - 1:
            xa = jnp.dot(lhs_s[rb * RB:RT + HALO_ROWS, :], w_ref[:, 0:D_MODEL],
                         preferred_element_type=F32)
            xa_s[0:BATCH, :] = xa[RB:RB + BATCH, :] * prev_valid
            xa_s[BATCH + RT:3 * BATCH + RT, :] = xa[RB + BATCH:RB + 3 * BATCH, :] * next_valid
            xa_s[BATCH + rb * RB:BATCH + RT, :] = xa[0:RB, :]
        else:
            xa_s[BATCH + rb * RB:BATCH + (rb + 1) * RB, :] = jnp.dot(
                lhs_s[rows, :], w_ref[:, 0:D_MODEL], preferred_element_type=F32)
        if latent:
            ga = jnp.dot(lhs_s[rows, :], w_ref[:, D_MODEL:2 * D_MODEL], preferred_element_type=F32)
            sga_ref[rows, :] = _silu(ga).astype(BF16)

    def recur(rb):
        r0 = rb * RB
        acc = xa_s[r0:r0 + RB, :] * cw[0:1, :]
        for k in range(1, 4):
            acc = acc + xa_s[r0 + k * BATCH:r0 + k * BATCH + RB, :] * cw[k:k + 1, :]
        xc = acc + cb
        xcb = xc.astype(BF16)
        xc_ref[r0:r0 + RB, :] = xcb
        for h in range(LRU_HEADS):
            cols = slice(h * HEAD_DIM, (h + 1) * HEAD_DIM)
            a, u = _lru_gates(xcb[:, cols], xc[:, cols], wg_ref[h], hba[:, cols], hbx[:, cols],
                              k_all[:, cols])
            for t in reversed(range(TB)):
                step = slice(t * BATCH, (t + 1) * BATCH)
                hcur[h] = a[step, :] * hcur[h] + u[step, :]
                if latent:
                    hb_ref[r0 + t * BATCH:r0 + (t + 1) * BATCH, cols] = hcur[h]

    for rb in reversed(range(N_RB)):
        project(rb)
        if rb + 1 < N_RB:
            recur(rb + 1)
    recur(0)

    for h in range(LRU_HEADS):
        cols = slice(h * HEAD_DIM, (h + 1) * HEAD_DIM)
        carry_s[:, cols] = hcur[h]
        hfin_ref[:, cols] = hcur[h]


def _pre_bwd_call(x, mod, norm_g, w, conv_w, conv_b, w_gate, ba, bx, lam, h0, latent):
    _, seq, _ = x.shape
    n = seq // TL
    mod_rows = 0 if latent else 1
    tile = lambda j: (n - 1 - j, 0)
    out_shape = [jax.ShapeDtypeStruct((seq * BATCH, D_MODEL), BF16),
                 jax.ShapeDtypeStruct((BATCH, D_MODEL), F32)]
    out_specs = [pl.BlockSpec((RT, D_MODEL), tile),
                 pl.BlockSpec((BATCH, D_MODEL), lambda j: (0, 0))]
    if latent:
        out_shape += [jax.ShapeDtypeStruct((seq * BATCH, D_MODEL), F32),
                      jax.ShapeDtypeStruct((seq * BATCH, D_MODEL), BF16)]
        out_specs += [pl.BlockSpec((RT, D_MODEL), tile), pl.BlockSpec((RT, D_MODEL), tile)]
    return pl.pallas_call(
        functools.partial(_pre_bwd_kernel, latent),
        grid=(n,),
        in_specs=[
            pl.BlockSpec(memory_space=pl.ANY),
            _mod_spec(mod_rows, 0),
            _mod_spec(mod_rows, 1),
            _const_spec((1, D_MODEL)),
            pl.BlockSpec((D_MODEL, (2 if latent else 1) * D_MODEL), lambda j: (0, 0),
                         pipeline_mode=pl.Buffered(1)),
            _const_spec((4, D_MODEL)),
            _const_spec((1, D_MODEL)),
            _const_spec((LRU_HEADS, HEAD_DIM, 2 * HEAD_DIM)),
            _const_spec((1, D_MODEL)),
            _const_spec((1, D_MODEL)),
            _const_spec((1, D_MODEL)),
            _const_spec((BATCH, D_MODEL)),
        ],
        out_specs=out_specs,
        out_shape=out_shape,
        scratch_shapes=[
            pltpu.VMEM((2, TL + HALO_T, BATCH, D_MODEL), F32),
            pltpu.SemaphoreType.DMA((2,)),
            pltpu.VMEM((RT + HALO_ROWS, D_MODEL), BF16),
            pltpu.VMEM((RT + 3 * BATCH, D_MODEL), F32),
            pltpu.VMEM((BATCH, D_MODEL), F32),
        ],
        compiler_params=pltpu.CompilerParams(dimension_semantics=("arbitrary",),
                                             vmem_limit_bytes=VMEM_LIMIT),
        name="lru_pre_bwd" if latent else "ctx_pre_bwd",
    )(x, mod, mod, norm_g, w, conv_w, conv_b, w_gate, ba, bx, lam, h0)


def _scan_kernel(combine, xc_ref, w_ref, ba_ref, bx_ref, lam_ref, h0_ref, *rest):
    if combine:
        hb_ref, sga_ref, hfin_ref, y_hbm, carry_s, a_s, u_s, h_s, ybuf, sem = rest
    else:
        hfin_ref, carry_s, a_s, u_s = rest

    j = pl.program_id(0)
    n = pl.num_programs(0)
    slot = j % 2

    @pl.when(j == 0)
    def _():
        carry_s[...] = h0_ref[...]

    def out_copies(tile, buf_slot):
        return [pltpu.make_async_copy(ybuf.at[buf_slot, :, b, :],
                                      y_hbm.at[b, pl.ds(tile * TL, TL), :],
                                      sem.at[buf_slot]) for b in range(BATCH)]

    k_all = _lru_coef(lam_ref[...])
    hba = 0.5 * ba_ref[...]
    hbx = 0.5 * bx_ref[...]
    for h in range(LRU_HEADS):
        cols = slice(h * HEAD_DIM, (h + 1) * HEAD_DIM)
        xb = xc_ref[:, cols]
        a, u = _lru_gates(xb, xb.astype(F32), w_ref[h], hba[:, cols], hbx[:, cols], k_all[:, cols])
        a_s[:, cols] = a
        u_s[:, cols] = u

    def step(kk, hcur):
        row = pl.multiple_of(kk * BATCH, BATCH)
        hnew = a_s[pl.ds(row, BATCH), :] * hcur + u_s[pl.ds(row, BATCH), :]
        if combine:
            h_s[pl.ds(row, BATCH), :] = hnew
        return hnew

    hlast = lax.fori_loop(0, TL, step, carry_s[...], unroll=8)
    carry_s[...] = hlast
    hfin_ref[...] = hlast

    if combine:
        @pl.when(j >= 2)
        def _():
            for cp in out_copies(j - 2, slot):
                cp.wait()

        y = (h_s[...] + hb_ref[...]) * sga_ref[...].astype(F32)
        ybuf[slot] = y.reshape(TL, BATCH, D_MODEL)
        for cp in out_copies(j, slot):
            cp.start()

        @pl.when(j == n - 1)
        def _():
            for cp in out_copies(j, slot):
                cp.wait()

        @pl.when(jnp.logical_and(j == n - 1, j >= 1))
        def _():
            for cp in out_copies(j - 1, 1 - slot):
                cp.wait()


def _scan_call(xc, w, ba, bx, lam, h0, hb=None, sga=None, *, combine):
    rows = xc.shape[0]
    seq = rows // BATCH
    n = seq // TL
    tile = lambda j: (j, 0)
    in_specs = [
        pl.BlockSpec((RT, D_MODEL), tile),
        _const_spec((LRU_HEADS, HEAD_DIM, 2 * HEAD_DIM)),
        _const_spec((1, D_MODEL)),
        _const_spec((1, D_MODEL)),
        _const_spec((1, D_MODEL)),
        _const_spec((BATCH, D_MODEL)),
    ]
    args = [xc, w, ba, bx, lam, h0]
    if combine:
        in_specs += [pl.BlockSpec((RT, D_MODEL), tile), pl.BlockSpec((RT, D_MODEL), tile)]
        args += [hb, sga]
    out_shape = [jax.ShapeDtypeStruct((BATCH, D_MODEL), F32)]
    out_specs = [pl.BlockSpec((BATCH, D_MODEL), lambda j: (0, 0))]
    if combine:
        out_shape.append(jax.ShapeDtypeStruct((BATCH, seq, D_MODEL), F32))
        out_specs.append(pl.BlockSpec(memory_space=pl.ANY))
    scratch = [pltpu.VMEM((BATCH, D_MODEL), F32),
               pltpu.VMEM((RT, D_MODEL), F32),
               pltpu.VMEM((RT, D_MODEL), F32)]
    if combine:
        scratch.append(pltpu.VMEM((RT, D_MODEL), F32))
        scratch.append(pltpu.VMEM((2, TL, BATCH, D_MODEL), F32))
        scratch.append(pltpu.SemaphoreType.DMA((2,)))
    return pl.pallas_call(
        functools.partial(_scan_kernel, combine),
        grid=(n,),
        in_specs=in_specs,
        out_specs=out_specs,
        out_shape=out_shape,
        scratch_shapes=scratch,
        compiler_params=pltpu.CompilerParams(dimension_semantics=("arbitrary",),
                                             vmem_limit_bytes=VMEM_LIMIT),
        name="lru_scan_fwd" + ("_combine" if combine else ""),
    )(*args)


def _final_kernel(x_ref, sh_ref, sc_ref, gx_ref, g_ref, w_ref, lng_ref, lnb_ref, ws_ref, bs_ref,
                  yl_ref, wo_ref, fg_ref, o_ref, vn_s, y_s):
    x = x_ref[...]
    gain = g_ref[...] * (1.0 + sc_ref[...])
    hn = _rms_scale(x) * gain[:, None, :] + sh_ref[...][:, None, :]
    lhs = hn.reshape(RT, D_MODEL).astype(BF16)

    v = _gelu_tanh(jnp.dot(lhs, w_ref[:, D_MODEL:2 * D_MODEL], preferred_element_type=F32))
    mu = jnp.mean(v, axis=-1, keepdims=True)
    vc = v - mu
    var = jnp.mean(vc * vc, axis=-1, keepdims=True)
    vn_s[...] = (vc * lax.rsqrt(var + LN_EPS) * lng_ref[...] + lnb_ref[...]).astype(BF16)

    u = _gelu_tanh(jnp.dot(lhs, w_ref[:, 0:D_MODEL], preferred_element_type=F32))
    gb = _silu(jnp.dot(lhs, w_ref[:, 2 * D_MODEL:3 * D_MODEL], preferred_element_type=F32))
    ug = u * gb

    y_s[:, 0:D_MODEL] = yl_ref[...].reshape(RT, D_MODEL).astype(BF16)
    for gidx in range(SGU_GROUPS):
        cols = slice(gidx * LANES, (gidx + 1) * LANES)
        rhs = jnp.concatenate([vn_s[b * TL:(b + 1) * TL, cols] for b in range(BATCH)], axis=1)
        mixed = jnp.dot(ws_ref[gidx], rhs, preferred_element_type=F32)
        for b in range(BATCH):
            rows = slice(b * TL, (b + 1) * TL)
            m_b = mixed[:, b * LANES:(b + 1) * LANES] + bs_ref[gidx]
            y_s[rows, D_MODEL + gidx * LANES:D_MODEL + (gidx + 1) * LANES] = (
                ug[rows, cols] * m_b).astype(BF16)

    mix = jnp.dot(y_s[...], wo_ref[...], preferred_element_type=F32)
    xnew = x + gx_ref[...][:, None, :] * mix.reshape(BATCH, TL, D_MODEL)
    o_ref[...] = _rms_scale(xnew) * fg_ref[...]


def _final_call(x, mod, norm_g, w_uvg, ln_g, ln_b, w_s, b_s, ylg, w_out, final_g):
    _, seq, _ = x.shape
    n = seq // TL
    tile = pl.BlockSpec((BATCH, TL, D_MODEL), lambda j: (0, j, 0))
    return pl.pallas_call(
        _final_kernel,
        grid=(n,),
        in_specs=[
            tile,
            _mod_spec(0, 0),
            _mod_spec(0, 1),
            _mod_spec(0, 2),
            _const_spec((1, D_MODEL)),
            _const_spec(w_uvg.shape),
            _const_spec((1, D_MODEL)),
            _const_spec((1, D_MODEL)),
            _const_spec((SGU_GROUPS, CHUNK, CHUNK)),
            _const_spec((SGU_GROUPS, CHUNK, LANES)),
            tile,
            _const_spec(w_out.shape),
            _const_spec((1, D_MODEL)),
        ],
        out_specs=tile,
        out_shape=jax.ShapeDtypeStruct(x.shape, F32),
        scratch_shapes=[pltpu.VMEM((RT, D_MODEL), BF16),
                        pltpu.VMEM((RT, 2 * D_MODEL), BF16)],
        compiler_params=pltpu.CompilerParams(dimension_semantics=("arbitrary",),
                                             vmem_limit_bytes=VMEM_LIMIT),
        name="sgu_out_final",
    )(x, mod, mod, mod, norm_g, w_uvg, ln_g, ln_b, w_s, b_s, ylg, w_out, final_g)


def kernel(x, c, ctx, c_ctx, ada_w, ada_b, norm_g, w_in, conv_w, conv_b, lru_wa, lru_ba,
           lru_wx, lru_bx, lru_lambda, sgu_ln_g, sgu_ln_b, sgu_w, sgu_b, w_out, final_g):
    assert x.shape == (BATCH, x.shape[1], D_MODEL) and x.shape[1] % TL == 0
    assert ctx.shape == (BATCH, ctx.shape[1], D_MODEL) and ctx.shape[1] % TL == 0
    assert ada_w.shape[0] == 1, "single-layer block"
    layer = 0
    W = D_MODEL

    cond = jnp.concatenate([c, c_ctx[None, :], jnp.zeros((BATCH - 1, D_MODEL), F32)], axis=0)
    mod = _ada_call(cond, ada_w[layer], ada_b[layer][None, :])

    ng = norm_g[layer][None, :]
    w_lru = w_in[layer][:, :2 * W].astype(BF16)
    w_uvg = w_in[layer][:, 2 * W:].astype(BF16)
    cw = conv_w[layer]
    cb = conv_b[layer][None, :]
    w_gate = jnp.concatenate([lru_wa[layer], lru_wx[layer]], axis=-1).astype(BF16)
    ba = lru_ba[layer].reshape(2, 1, W)
    bx = lru_bx[layer].reshape(2, 1, W)
    lam = lru_lambda[layer].reshape(2, 1, W)
    zeros = jnp.zeros((BATCH, W), F32)
    FWD, BWD = 0, 1

    def pre_bwd(inp, h0, latent):
        return _pre_bwd_call(inp, mod, ng, w_lru, cw, cb, w_gate[BWD], ba[BWD], bx[BWD], lam[BWD],
                             h0, latent)

    def scan_fwd(xc, h0, **kw):
        return _scan_call(xc, w_gate[FWD], ba[FWD], bx[FWD], lam[FWD], h0, **kw)

    xc_c, hb_c = pre_bwd(ctx, zeros, latent=False)
    (hf_c,) = scan_fwd(xc_c, zeros, combine=False)

    xc, _, hb, sga = pre_bwd(x, hb_c, latent=True)
    _, ylg = scan_fwd(xc, hf_c, hb=hb, sga=sga, combine=True)

    bs = jnp.broadcast_to(sgu_b[layer][:, :, None], (SGU_GROUPS, CHUNK, LANES))
    return _final_call(x, mod, ng, w_uvg, sgu_ln_g[layer][None, :], sgu_ln_b[layer][None, :],
                       sgu_w[layer].astype(BF16), bs, ylg, w_out[layer].astype(BF16),
                       final_g[None, :])
```

```python
import functools
import math

import jax
import jax.numpy as jnp
from jax import lax
from jax.experimental import pallas as pl
from jax.experimental.pallas import tpu as pltpu

F32 = jnp.float32
BF16 = jnp.bfloat16

D_MODEL = 1024
BATCH = 8
LRU_HEADS = 8
HEAD_DIM = 128
SGU_GROUPS = 8
CHUNK = 128
LRU_C = 8.0
NORM_EPS = 1e-6
LN_EPS = 1e-5

LANES = 128
N_SLABS = D_MODEL // LANES
TL = CHUNK
RT = TL * BATCH
TB = 32
RB = TB * BATCH
N_RB = TL // TB
HALO_T = 4
HALO_ROWS = HALO_T * BATCH
VMEM_LIMIT = 56 * 1024 * 1024


def _silu(x):
    h = 0.5 * x
    return h * jnp.tanh(h) + h


def _gelu_tanh(x):
    c = 0.7978845608028654
    h = 0.5 * x
    return h * jnp.tanh(x * ((c * 0.044715) * (x * x) + c)) + h


def _rms_scale(x):
    return x * lax.rsqrt(jnp.mean(x * x, axis=-1, keepdims=True) + NORM_EPS)


def _lru_coef(lam):
    z = -lam
    return (-0.5 * LRU_C * math.log2(math.e)) * (
        jnp.maximum(z, 0.0) + jnp.log1p(jnp.exp(-jnp.abs(z))))


def _lru_gates(xb, xh, w_h, hba, hbx, k):
    zz = jnp.dot(xb, w_h, preferred_element_type=F32)
    tr = jnp.tanh(zz[:, 0:HEAD_DIM] + hba)
    ti = jnp.tanh(zz[:, HEAD_DIM:2 * HEAD_DIM] + hbx)
    a = jnp.exp2(k * tr + k)
    om = 1.0 - a * a
    root = jnp.where(om > 0.0, om * lax.rsqrt(om), 0.0)
    return a, ((ti + 1.0) * xh) * root


def _const_spec(shape):
    zeros = (0,) * len(shape)
    return pl.BlockSpec(shape, lambda j: zeros, pipeline_mode=pl.Buffered(1))


def _mod_spec(row_block, col_block):
    return pl.BlockSpec((BATCH, D_MODEL), lambda j: (row_block, col_block),
                        pipeline_mode=pl.Buffered(1))


def _ada_kernel(cond_ref, w_ref, b_ref, o_ref):
    s = _silu(cond_ref[...])
    o_ref[...] = jnp.dot(s, w_ref[...], preferred_element_type=F32,
                         precision=lax.Precision.HIGHEST) + b_ref[...]


def _ada_call(cond, w, b):
    rows = cond.shape[0]
    n_out = w.shape[1]
    tn = D_MODEL
    return pl.pallas_call(
        _ada_kernel,
        grid=(n_out // tn,),
        in_specs=[pl.BlockSpec((rows, D_MODEL), lambda j: (0, 0)),
                  pl.BlockSpec((D_MODEL, tn), lambda j: (0, j)),
                  pl.BlockSpec((1, tn), lambda j: (0, j))],
        out_specs=pl.BlockSpec((rows, tn), lambda j: (0, j)),
        out_shape=jax.ShapeDtypeStruct((rows, n_out), F32),
        compiler_params=pltpu.CompilerParams(dimension_semantics=("arbitrary",)),
        name="ada_mod",
    )(cond, w, b)


def _pre_bwd_kernel(latent, x_hbm, sh_ref, sc_ref, g_ref, w_ref, cw_ref, cb_ref,
                    wg_ref, ba_ref, bx_ref, lam_ref, h0_ref, *rest):
    if latent:
        xc_ref, hfin_ref, hb_ref, sga_ref, xbuf, sem, lhs_s, xa_s, carry_s = rest
        sc = sc_ref[...]
        sh = sh_ref[...]
    else:
        xc_ref, hfin_ref, xbuf, sem, lhs_s, xa_s, carry_s = rest
        sc = sc_ref[0:1, :]
        sh = sh_ref[0:1, :]
    j = pl.program_id(0)
    n = pl.num_programs(0)
    seq = n * TL
    slot = j % 2

    def tile_copies(tile, buf_slot):
        t0 = tile * TL
        copies = []
        for b in range(BATCH):
            for src_t, dst_t, cnt in ((t0, 0, TL),
                                      (jnp.maximum(t0 - 1, 0), TL, 1),
                                      (jnp.minimum(t0 + TL, seq - 2), TL + 1, 2)):
                copies.append(pltpu.make_async_copy(
                    x_hbm.at[b, pl.ds(src_t, cnt), :],
                    xbuf.at[buf_slot, pl.ds(dst_t, cnt), b, :],
                    sem.at[buf_slot]))
        return copies

    @pl.when(j == 0)
    def _():
        carry_s[...] = h0_ref[...]
        xbuf[:, TL + HALO_T - 1:TL + HALO_T, :, :] = jnp.zeros((2, 1, BATCH, D_MODEL), F32)
        for cp in tile_copies(n - 1, 0):
            cp.start()

    for cp in tile_copies(n - 1 - j, slot):
        cp.wait()

    @pl.when(j + 1 < n)
    def _():
        for cp in tile_copies(n - 2 - j, 1 - slot):
            cp.start()

    gain = g_ref[...] * (1.0 + sc)

    def norm_rows(xv):
        hn = _rms_scale(xv) * gain[None, :, :] + sh[None, :, :]
        return hn.reshape(xv.shape[0] * BATCH, D_MODEL).astype(BF16)

    lhs_s[RT:RT + HALO_ROWS, :] = norm_rows(xbuf[slot, TL:TL + HALO_T])
    prev_valid = (j < n - 1).astype(F32)
    next_valid = (j > 0).astype(F32)

    cw = 0.5 * cw_ref[...]
    cb = 0.5 * cb_ref[...]
    k_all = _lru_coef(lam_ref[...])
    hba = 0.5 * ba_ref[...]
    hbx = 0.5 * bx_ref[...]
    hcur = [carry_s[:, h * HEAD_DIM:(h + 1) * HEAD_DIM] for h in range(LRU_HEADS)]

    def project(rb):
        t0 = rb * TB
        rows = slice(rb * RB, (rb + 1) * RB)
        lhs_s[rows, :] = norm_rows(xbuf[slot, t0:t0 + TB])
        if rb == N_RB - 1:
            xa = jnp.dot(lhs_s[rb * RB:RT + HALO_ROWS, :], w_ref[:, 0:D_MODEL],
                         preferred_element_type=F32)
            xa_s[0:BATCH, :] = xa[RB:RB + BATCH, :] * prev_valid
            xa_s[BATCH + RT:3 * BATCH + RT, :] = xa[RB + BATCH:RB + 3 * BATCH, :] * next_valid
            xa_s[BATCH + rb * RB:BATCH + RT, :] = xa[0:RB, :]
        else:
            xa_s[BATCH + rb * RB:BATCH + (rb + 1) * RB, :] = jnp.dot(
                lhs_s[rows, :], w_ref[:, 0:D_MODEL], preferred_element_type=F32)
        if latent:
            ga = jnp.dot(lhs_s[rows, :], w_ref[:, D_MODEL:2 * D_MODEL], preferred_element_type=F32)
            sga_ref[rows, :] = _silu(ga).astype(BF16)

    def recur(rb):
        r0 = rb * RB
        acc = xa_s[r0:r0 + RB, :] * cw[0:1, :]
        for k in range(1, 4):
            acc = acc + xa_s[r0 + k * BATCH:r0 + k * BATCH + RB, :] * cw[k:k + 1, :]
        xc = acc + cb
        xcb = xc.astype(BF16)
        xc_ref[r0:r0 + RB, :] = xcb
        for h in range(LRU_HEADS):
            cols = slice(h * HEAD_DIM, (h + 1) * HEAD_DIM)
            a, u = _lru_gates(xcb[:, cols], xc[:, cols], wg_ref[h], hba[:, cols], hbx[:, cols],
                              k_all[:, cols])
            for t in reversed(range(TB)):
                step = slice(t * BATCH, (t + 1) * BATCH)
                h_later = hcur[h]
                hcur[h] = a[step, :] * hcur[h] + u[step, :]
                if latent and t % 2 == 0:
                    hb_ref[r0 + t * BATCH:r0 + (t + 2) * BATCH, cols] = (
                        jnp.concatenate([hcur[h], h_later], axis=0).astype(BF16))

    for rb in reversed(range(N_RB)):
        project(rb)
        if rb + 1 < N_RB:
            recur(rb + 1)
    recur(0)

    for h in range(LRU_HEADS):
        cols = slice(h * HEAD_DIM, (h + 1) * HEAD_DIM)
        carry_s[:, cols] = hcur[h]
        hfin_ref[:, cols] = hcur[h]


def _pre_bwd_call(x, mod, norm_g, w, conv_w, conv_b, w_gate, ba, bx, lam, h0, latent):
    _, seq, _ = x.shape
    n = seq // TL
    mod_rows = 0 if latent else 1
    tile = lambda j: (n - 1 - j, 0)
    out_shape = [jax.ShapeDtypeStruct((seq * BATCH, D_MODEL), BF16),
                 jax.ShapeDtypeStruct((BATCH, D_MODEL), F32)]
    out_specs = [pl.BlockSpec((RT, D_MODEL), tile),
                 pl.BlockSpec((BATCH, D_MODEL), lambda j: (0, 0))]
    if latent:
        out_shape += [jax.ShapeDtypeStruct((seq * BATCH, D_MODEL), BF16),
                      jax.ShapeDtypeStruct((seq * BATCH, D_MODEL), BF16)]
        out_specs += [pl.BlockSpec((RT, D_MODEL), tile), pl.BlockSpec((RT, D_MODEL), tile)]
    return pl.pallas_call(
        functools.partial(_pre_bwd_kernel, latent),
        grid=(n,),
        in_specs=[
            pl.BlockSpec(memory_space=pl.ANY),
            _mod_spec(mod_rows, 0),
            _mod_spec(mod_rows, 1),
            _const_spec((1, D_MODEL)),
            pl.BlockSpec((D_MODEL, (2 if latent else 1) * D_MODEL), lambda j: (0, 0),
                         pipeline_mode=pl.Buffered(1)),
            _const_spec((4, D_MODEL)),
            _const_spec((1, D_MODEL)),
            _const_spec((LRU_HEADS, HEAD_DIM, 2 * HEAD_DIM)),
            _const_spec((1, D_MODEL)),
            _const_spec((1, D_MODEL)),
            _const_spec((1, D_MODEL)),
            _const_spec((BATCH, D_MODEL)),
        ],
        out_specs=out_specs,
        out_shape=out_shape,
        scratch_shapes=[
            pltpu.VMEM((2, TL + HALO_T, BATCH, D_MODEL), F32),
            pltpu.SemaphoreType.DMA((2,)),
            pltpu.VMEM((RT + HALO_ROWS, D_MODEL), BF16),
            pltpu.VMEM((RT + 3 * BATCH, D_MODEL), F32),
            pltpu.VMEM((BATCH, D_MODEL), F32),
        ],
        compiler_params=pltpu.CompilerParams(dimension_semantics=("arbitrary",),
                                             vmem_limit_bytes=VMEM_LIMIT),
        name="lru_pre_bwd" if latent else "ctx_pre_bwd",
    )(x, mod, mod, norm_g, w, conv_w, conv_b, w_gate, ba, bx, lam, h0)


def _scan_kernel(combine, xc_ref, w_ref, ba_ref, bx_ref, lam_ref, h0_ref, *rest):
    if combine:
        hb_ref, sga_ref, hfin_ref, y_hbm, carry_s, a_s, u_s, h_s, ybuf, sem = rest
    else:
        hfin_ref, carry_s, a_s, u_s = rest

    j = pl.program_id(0)
    n = pl.num_programs(0)
    slot = j % 2

    @pl.when(j == 0)
    def _():
        carry_s[...] = h0_ref[...]

    def out_copies(tile, buf_slot):
        return [pltpu.make_async_copy(ybuf.at[buf_slot, :, b, :],
                                      y_hbm.at[b, pl.ds(tile * TL, TL), :],
                                      sem.at[buf_slot]) for b in range(BATCH)]

    k_all = _lru_coef(lam_ref[...])
    hba = 0.5 * ba_ref[...]
    hbx = 0.5 * bx_ref[...]
    for h in range(LRU_HEADS):
        cols = slice(h * HEAD_DIM, (h + 1) * HEAD_DIM)
        xb = xc_ref[:, cols]
        a, u = _lru_gates(xb, xb.astype(F32), w_ref[h], hba[:, cols], hbx[:, cols], k_all[:, cols])
        a_s[:, cols] = a
        u_s[:, cols] = u

    def step(kk, hcur):
        row = pl.multiple_of(kk * BATCH, BATCH)
        hnew = a_s[pl.ds(row, BATCH), :] * hcur + u_s[pl.ds(row, BATCH), :]
        if combine:
            h_s[pl.ds(row, BATCH), :] = hnew
        return hnew

    hlast = lax.fori_loop(0, TL, step, carry_s[...], unroll=8)
    carry_s[...] = hlast
    hfin_ref[...] = hlast

    if combine:
        @pl.when(j >= 2)
        def _():
            for cp in out_copies(j - 2, slot):
                cp.wait()

        y = (h_s[...] + hb_ref[...]) * sga_ref[...].astype(F32)
        ybuf[slot] = y.reshape(TL, BATCH, D_MODEL)
        for cp in out_copies(j, slot):
            cp.start()

        @pl.when(j == n - 1)
        def _():
            for cp in out_copies(j, slot):
                cp.wait()

        @pl.when(jnp.logical_and(j == n - 1, j >= 1))
        def _():
            for cp in out_copies(j - 1, 1 - slot):
                cp.wait()


def _scan_call(xc, w, ba, bx, lam, h0, hb=None, sga=None, *, combine):
    rows = xc.shape[0]
    seq = rows // BATCH
    n = seq // TL
    tile = lambda j: (j, 0)
    in_specs = [
        pl.BlockSpec((RT, D_MODEL), tile),
        _const_spec((LRU_HEADS, HEAD_DIM, 2 * HEAD_DIM)),
        _const_spec((1, D_MODEL)),
        _const_spec((1, D_MODEL)),
        _const_spec((1, D_MODEL)),
        _const_spec((BATCH, D_MODEL)),
    ]
    args = [xc, w, ba, bx, lam, h0]
    if combine:
        in_specs += [pl.BlockSpec((RT, D_MODEL), tile), pl.BlockSpec((RT, D_MODEL), tile)]
        args += [hb, sga]
    out_shape = [jax.ShapeDtypeStruct((BATCH, D_MODEL), F32)]
    out_specs = [pl.BlockSpec((BATCH, D_MODEL), lambda j: (0, 0))]
    if combine:
        out_shape.append(jax.ShapeDtypeStruct((BATCH, seq, D_MODEL), F32))
        out_specs.append(pl.BlockSpec(memory_space=pl.ANY))
    scratch = [pltpu.VMEM((BATCH, D_MODEL), F32),
               pltpu.VMEM((RT, D_MODEL), F32),
               pltpu.VMEM((RT, D_MODEL), F32)]
    if combine:
        scratch.append(pltpu.VMEM((RT, D_MODEL), F32))
        scratch.append(pltpu.VMEM((2, TL, BATCH, D_MODEL), F32))
        scratch.append(pltpu.SemaphoreType.DMA((2,)))
    return pl.pallas_call(
        functools.partial(_scan_kernel, combine),
        grid=(n,),
        in_specs=in_specs,
        out_specs=out_specs,
        out_shape=out_shape,
        scratch_shapes=scratch,
        compiler_params=pltpu.CompilerParams(dimension_semantics=("arbitrary",),
                                             vmem_limit_bytes=VMEM_LIMIT),
        name="lru_scan_fwd" + ("_combine" if combine else ""),
    )(*args)


N_TILE = 2 * HEAD_DIM


def _fwd_mix_kernel(xc_ref, w_ref, ba_ref, bx_ref, lam_ref, h0_ref, hb_ref, sga_ref, wo_ref,
                    o_hbm, carry_s, a_s, u_s, h_s, ylhs_s, obuf, sem):
    j = pl.program_id(0)
    n = pl.num_programs(0) - 1
    slot = j % 2

    @pl.when(j == 0)
    def _():
        carry_s[...] = h0_ref[...]
        ylhs_s[...] = jnp.zeros_like(ylhs_s)

    def out_copies(tile, buf_slot):
        return [pltpu.make_async_copy(obuf.at[buf_slot, :, b, :],
                                      o_hbm.at[b, pl.ds(tile * TL, TL), :],
                                      sem.at[buf_slot]) for b in range(BATCH)]

    @pl.when(j >= 3)
    def _():
        for cp in out_copies(j - 3, slot):
            cp.wait()

    k_all = _lru_coef(lam_ref[...])
    hba = 0.5 * ba_ref[...]
    hbx = 0.5 * bx_ref[...]
    for p in range(D_MODEL // N_TILE):
        for h in (2 * p, 2 * p + 1):
            cols = slice(h * HEAD_DIM, (h + 1) * HEAD_DIM)
            xb = xc_ref[:, cols]
            a, u = _lru_gates(xb, xb.astype(F32), w_ref[h], hba[:, cols], hbx[:, cols],
                              k_all[:, cols])
            a_s[:, cols] = a
            u_s[:, cols] = u
        ncols = slice(p * N_TILE, (p + 1) * N_TILE)
        proj = jnp.dot(ylhs_s[...], wo_ref[:, ncols], preferred_element_type=F32)
        obuf[slot, :, :, ncols] = proj.reshape(TL, BATCH, N_TILE)

    def step(kk, hcur):
        row = pl.multiple_of(kk * BATCH, BATCH)
        hnew = a_s[pl.ds(row, BATCH), :] * hcur + u_s[pl.ds(row, BATCH), :]
        h_s[pl.ds(row, BATCH), :] = hnew
        return hnew

    carry_s[...] = lax.fori_loop(0, TL, step, carry_s[...], unroll=8)

    ylhs_s[...] = ((h_s[...] + hb_ref[...].astype(F32)) * sga_ref[...].astype(F32)).astype(BF16)

    @pl.when(j >= 1)
    def _():
        for cp in out_copies(j - 1, slot):
            cp.start()

    @pl.when(j == n)
    def _():
        for cp in out_copies(j - 1, slot):
            cp.wait()

    @pl.when(jnp.logical_and(j == n, j >= 2))
    def _():
        for cp in out_copies(j - 2, 1 - slot):
            cp.wait()


def _fwd_mix_call(xc, w, ba, bx, lam, h0, hb, sga, w_out_lru):
    rows = xc.shape[0]
    seq = rows // BATCH
    n = seq // TL
    tile = pl.BlockSpec((RT, D_MODEL), lambda j: (jnp.minimum(j, n - 1), 0))
    return pl.pallas_call(
        _fwd_mix_kernel,
        grid=(n + 1,),
        in_specs=[
            tile,
            _const_spec((LRU_HEADS, HEAD_DIM, 2 * HEAD_DIM)),
            _const_spec((1, D_MODEL)),
            _const_spec((1, D_MODEL)),
            _const_spec((1, D_MODEL)),
            _const_spec((BATCH, D_MODEL)),
            tile,
            tile,
            _const_spec((D_MODEL, D_MODEL)),
        ],
        out_specs=pl.BlockSpec(memory_space=pl.ANY),
        out_shape=jax.ShapeDtypeStruct((BATCH, seq, D_MODEL), F32),
        scratch_shapes=[
            pltpu.VMEM((BATCH, D_MODEL), F32),
            pltpu.VMEM((RT, D_MODEL), F32),
            pltpu.VMEM((RT, D_MODEL), F32),
            pltpu.VMEM((RT, D_MODEL), F32),
            pltpu.VMEM((RT, D_MODEL), BF16),
            pltpu.VMEM((2, TL, BATCH, D_MODEL), F32),
            pltpu.SemaphoreType.DMA((2,)),
        ],
        compiler_params=pltpu.CompilerParams(dimension_semantics=("arbitrary",),
                                             vmem_limit_bytes=VMEM_LIMIT),
        name="lru_fwd_mix",
    )(xc, w, ba, bx, lam, h0, hb, sga, w_out_lru)


def _final_kernel(x_ref, sh_ref, sc_ref, gx_ref, g_ref, w_ref, lng_ref, lnb_ref, ws_ref, bs_ref,
                  yl_ref, wo_ref, fg_ref, o_ref, vn_s, y_s):
    x = x_ref[...]
    gain = g_ref[...] * (1.0 + sc_ref[...])
    hn = _rms_scale(x) * gain[:, None, :] + sh_ref[...][:, None, :]
    lhs = hn.reshape(RT, D_MODEL).astype(BF16)

    v = _gelu_tanh(jnp.dot(lhs, w_ref[:, D_MODEL:2 * D_MODEL], preferred_element_type=F32))
    mu = jnp.mean(v, axis=-1, keepdims=True)
    vc = v - mu
    var = jnp.mean(vc * vc, axis=-1, keepdims=True)
    vn_s[...] = (vc * lax.rsqrt(var + LN_EPS) * lng_ref[...] + lnb_ref[...]).astype(BF16)

    u = _gelu_tanh(jnp.dot(lhs, w_ref[:, 0:D_MODEL], preferred_element_type=F32))
    gb = _silu(jnp.dot(lhs, w_ref[:, 2 * D_MODEL:3 * D_MODEL], preferred_element_type=F32))
    ug = u * gb

    for gidx in range(SGU_GROUPS):
        cols = slice(gidx * LANES, (gidx + 1) * LANES)
        rhs = jnp.concatenate([vn_s[b * TL:(b + 1) * TL, cols] for b in range(BATCH)], axis=1)
        mixed = jnp.dot(ws_ref[gidx], rhs, preferred_element_type=F32)
        for b in range(BATCH):
            rows = slice(b * TL, (b + 1) * TL)
            m_b = mixed[:, b * LANES:(b + 1) * LANES] + bs_ref[gidx]
            y_s[rows, cols] = (ug[rows, cols] * m_b).astype(BF16)

    mix = yl_ref[...] + jnp.dot(y_s[...], wo_ref[...], preferred_element_type=F32).reshape(
        BATCH, TL, D_MODEL)
    xnew = x + gx_ref[...][:, None, :] * mix
    o_ref[...] = _rms_scale(xnew) * fg_ref[...]


def _final_call(x, mod, norm_g, w_uvg, ln_g, ln_b, w_s, b_s, ylg, w_out, final_g):
    _, seq, _ = x.shape
    n = seq // TL
    tile = pl.BlockSpec((BATCH, TL, D_MODEL), lambda j: (0, j, 0))
    return pl.pallas_call(
        _final_kernel,
        grid=(n,),
        in_specs=[
            tile,
            _mod_spec(0, 0),
            _mod_spec(0, 1),
            _mod_spec(0, 2),
            _const_spec((1, D_MODEL)),
            _const_spec(w_uvg.shape),
            _const_spec((1, D_MODEL)),
            _const_spec((1, D_MODEL)),
            _const_spec((SGU_GROUPS, CHUNK, CHUNK)),
            _const_spec((SGU_GROUPS, CHUNK, LANES)),
            tile,
            _const_spec(w_out.shape),
            _const_spec((1, D_MODEL)),
        ],
        out_specs=tile,
        out_shape=jax.ShapeDtypeStruct(x.shape, F32),
        scratch_shapes=[pltpu.VMEM((RT, D_MODEL), BF16),
                        pltpu.VMEM((RT, D_MODEL), BF16)],
        compiler_params=pltpu.CompilerParams(dimension_semantics=("arbitrary",),
                                             vmem_limit_bytes=VMEM_LIMIT),
        name="sgu_out_final",
    )(x, mod, mod, mod, norm_g, w_uvg, ln_g, ln_b, w_s, b_s, ylg, w_out, final_g)


def kernel(x, c, ctx, c_ctx, ada_w, ada_b, norm_g, w_in, conv_w, conv_b, lru_wa, lru_ba,
           lru_wx, lru_bx, lru_lambda, sgu_ln_g, sgu_ln_b, sgu_w, sgu_b, w_out, final_g):
    assert x.shape == (BATCH, x.shape[1], D_MODEL) and x.shape[1] % TL == 0
    assert ctx.shape == (BATCH, ctx.shape[1], D_MODEL) and ctx.shape[1] % TL == 0
    assert ada_w.shape[0] == 1, "single-layer block"
    layer = 0
    W = D_MODEL

    cond = jnp.concatenate([c, c_ctx[None, :], jnp.zeros((BATCH - 1, D_MODEL), F32)], axis=0)
    mod = _ada_call(cond, ada_w[layer], ada_b[layer][None, :])

    ng = norm_g[layer][None, :]
    w_lru = w_in[layer][:, :2 * W].astype(BF16)
    w_uvg = w_in[layer][:, 2 * W:].astype(BF16)
    cw = conv_w[layer]
    cb = conv_b[layer][None, :]
    w_gate = jnp.concatenate([lru_wa[layer], lru_wx[layer]], axis=-1).astype(BF16)
    ba = lru_ba[layer].reshape(2, 1, W)
    bx = lru_bx[layer].reshape(2, 1, W)
    lam = lru_lambda[layer].reshape(2, 1, W)
    zeros = jnp.zeros((BATCH, W), F32)
    FWD, BWD = 0, 1

    def pre_bwd(inp, h0, latent):
        return _pre_bwd_call(inp, mod, ng, w_lru, cw, cb, w_gate[BWD], ba[BWD], bx[BWD], lam[BWD],
                             h0, latent)

    def scan_fwd(xc, h0, **kw):
        return _scan_call(xc, w_gate[FWD], ba[FWD], bx[FWD], lam[FWD], h0, **kw)

    xc_c, hb_c = pre_bwd(ctx, zeros, latent=False)
    (hf_c,) = scan_fwd(xc_c, zeros, combine=False)

    xc, _, hb, sga = pre_bwd(x, hb_c, latent=True)
    w_o = w_out[layer].astype(BF16)
    out_lru = _fwd_mix_call(xc, w_gate[FWD], ba[FWD], bx[FWD], lam[FWD], hf_c, hb, sga, w_o[:W])

    bs = jnp.broadcast_to(sgu_b[layer][:, :, None], (SGU_GROUPS, CHUNK, LANES))
    return _final_call(x, mod, ng, w_uvg, sgu_ln_g[layer][None, :], sgu_ln_b[layer][None, :],
                       sgu_w[layer].astype(BF16), bs, out_lru, w_o[W:], final_g[None, :])
```

```python
import functools
import math

import jax
import jax.numpy as jnp
from jax import lax
from jax.experimental import pallas as pl
from jax.experimental.pallas import tpu as pltpu

F32 = jnp.float32
BF16 = jnp.bfloat16

D_MODEL = 1024
BATCH = 8
LRU_HEADS = 8
HEAD_DIM = 128
SGU_GROUPS = 8
CHUNK = 128
LRU_C = 8.0
NORM_EPS = 1e-6
LN_EPS = 1e-5

LANES = 128
TL = CHUNK
RT = TL * BATCH
TB = 32
RB = TB * BATCH
N_RB = TL // TB
HALO_T = 4
HALO_ROWS = HALO_T * BATCH
VMEM_LIMIT = 56 * 1024 * 1024


def _silu(x):
    h = 0.5 * x
    return h * jnp.tanh(h) + h


def _gelu_tanh(x):
    c = 0.7978845608028654
    h = 0.5 * x
    return h * jnp.tanh(x * ((c * 0.044715) * (x * x) + c)) + h


def _rms_scale(x):
    return x * lax.rsqrt(jnp.mean(x * x, axis=-1, keepdims=True) + NORM_EPS)


def _lru_coef(lam):
    z = -lam
    return (-0.5 * LRU_C * math.log2(math.e)) * (
        jnp.maximum(z, 0.0) + jnp.log1p(jnp.exp(-jnp.abs(z))))


def _lru_gates(xb, xh, w_h, hba, hbx, k):
    zz = jnp.dot(xb, w_h, preferred_element_type=F32)
    tr = jnp.tanh(zz[:, 0:HEAD_DIM] + hba)
    ti = jnp.tanh(zz[:, HEAD_DIM:2 * HEAD_DIM] + hbx)
    a = jnp.exp2(k * tr + k)
    om = 1.0 - a * a
    root = jnp.where(om > 0.0, om * lax.rsqrt(om), 0.0)
    return a, ((ti + 1.0) * xh) * root


def _const_spec(shape):
    zeros = (0,) * len(shape)
    return pl.BlockSpec(shape, lambda j: zeros, pipeline_mode=pl.Buffered(1))


def _col_spec(rows, col_block, n_col_blocks=1):
    return pl.BlockSpec((rows, n_col_blocks * D_MODEL), lambda j: (0, col_block),
                        pipeline_mode=pl.Buffered(1))


def _mod_spec(row_block, col_block):
    return pl.BlockSpec((BATCH, D_MODEL), lambda j: (row_block, col_block),
                        pipeline_mode=pl.Buffered(1))


def _ada_kernel(cond_ref, w_ref, b_ref, o_ref):
    s = _silu(cond_ref[...])
    o_ref[...] = jnp.dot(s.astype(BF16), w_ref[...].astype(BF16),
                         preferred_element_type=F32) + b_ref[...]


def _ada_call(cond, w, b):
    rows = cond.shape[0]
    n_out = w.shape[1]
    tn = D_MODEL
    return pl.pallas_call(
        _ada_kernel,
        grid=(n_out // tn,),
        in_specs=[pl.BlockSpec((rows, D_MODEL), lambda j: (0, 0)),
                  pl.BlockSpec((D_MODEL, tn), lambda j: (0, j)),
                  pl.BlockSpec((1, tn), lambda j: (0, j))],
        out_specs=pl.BlockSpec((rows, tn), lambda j: (0, j)),
        out_shape=jax.ShapeDtypeStruct((rows, n_out), F32),
        compiler_params=pltpu.CompilerParams(dimension_semantics=("arbitrary",)),
        name="ada_mod",
    )(cond, w, b)


def _pre_bwd_kernel(latent, x_hbm, sh_ref, sc_ref, g_ref, w_ref, cw_ref, cb_ref,
                    wg_ref, ba_ref, bx_ref, lam_ref, h0_ref, *rest):
    if latent:
        xc_ref, hfin_ref, hb_ref, sga_ref, xbuf, sem, lhs_s, xa_s, carry_s = rest
        sc = sc_ref[...]
        sh = sh_ref[...]
    else:
        xc_ref, hfin_ref, xbuf, sem, lhs_s, xa_s, carry_s = rest
        sc = sc_ref[0:1, :]
        sh = sh_ref[0:1, :]
    j = pl.program_id(0)
    n = pl.num_programs(0)
    seq = n * TL
    slot = j % 2

    def tile_copies(tile, buf_slot):
        t0 = tile * TL
        copies = []
        for b in range(BATCH):
            for src_t, dst_t, cnt in ((t0, 0, TL),
                                      (jnp.maximum(t0 - 1, 0), TL, 1),
                                      (jnp.minimum(t0 + TL, seq - 2), TL + 1, 2)):
                copies.append(pltpu.make_async_copy(
                    x_hbm.at[b, pl.ds(src_t, cnt), :],
                    xbuf.at[buf_slot, pl.ds(dst_t, cnt), b, :],
                    sem.at[buf_slot]))
        return copies

    @pl.when(j == 0)
    def _():
        carry_s[...] = h0_ref[...]
        xbuf[:, TL + HALO_T - 1:TL + HALO_T, :, :] = jnp.zeros((2, 1, BATCH, D_MODEL), F32)
        for cp in tile_copies(n - 1, 0):
            cp.start()

    for cp in tile_copies(n - 1 - j, slot):
        cp.wait()

    @pl.when(j + 1 < n)
    def _():
        for cp in tile_copies(n - 2 - j, 1 - slot):
            cp.start()

    gain = g_ref[...] * (1.0 + sc)

    def norm_rows(xv):
        hn = _rms_scale(xv) * gain[None, :, :] + sh[None, :, :]
        return hn.reshape(xv.shape[0] * BATCH, D_MODEL).astype(BF16)

    lhs_s[RT:RT + HALO_ROWS, :] = norm_rows(xbuf[slot, TL:TL + HALO_T])
    prev_valid = (j < n - 1).astype(F32)
    next_valid = (j > 0).astype(F32)

    cw = 0.5 * cw_ref[...]
    cb = 0.5 * cb_ref[...]
    k_all = _lru_coef(lam_ref[...])
    hba = 0.5 * ba_ref[...]
    hbx = 0.5 * bx_ref[...]
    hcur = [carry_s[:, h * HEAD_DIM:(h + 1) * HEAD_DIM] for h in range(LRU_HEADS)]

    def project(rb):
        t0 = rb * TB
        rows = slice(rb * RB, (rb + 1) * RB)
        lhs_s[rows, :] = norm_rows(xbuf[slot, t0:t0 + TB])
        if rb == N_RB - 1:
            xa = jnp.dot(lhs_s[rb * RB:RT + HALO_ROWS, :], w_ref[:, 0:D_MODEL],
                         preferred_element_type=F32)
            xa_s[0:BATCH, :] = xa[RB:RB + BATCH, :] * prev_valid
            xa_s[BATCH + RT:3 * BATCH + RT, :] = xa[RB + BATCH:RB + 3 * BATCH, :] * next_valid
            xa_s[BATCH + rb * RB:BATCH + RT, :] = xa[0:RB, :]
        else:
            xa_s[BATCH + rb * RB:BATCH + (rb + 1) * RB, :] = jnp.dot(
                lhs_s[rows, :], w_ref[:, 0:D_MODEL], preferred_element_type=F32)
        if latent:
            ga = jnp.dot(lhs_s[rows, :], w_ref[:, D_MODEL:2 * D_MODEL], preferred_element_type=F32)
            sga_ref[rows, :] = _silu(ga).astype(BF16)

    def recur(rb):
        r0 = rb * RB
        acc = xa_s[r0:r0 + RB, :] * cw[0:1, :]
        for k in range(1, 4):
            acc = acc + xa_s[r0 + k * BATCH:r0 + k * BATCH + RB, :] * cw[k:k + 1, :]
        xc = acc + cb
        xcb = xc.astype(BF16)
        xc_ref[r0:r0 + RB, :] = xcb
        for h in range(LRU_HEADS):
            cols = slice(h * HEAD_DIM, (h + 1) * HEAD_DIM)
            a, u = _lru_gates(xcb[:, cols], xc[:, cols], wg_ref[h], hba[:, cols], hbx[:, cols],
                              k_all[:, cols])
            for t in reversed(range(TB)):
                step = slice(t * BATCH, (t + 1) * BATCH)
                h_later = hcur[h]
                hcur[h] = a[step, :] * hcur[h] + u[step, :]
                if latent and t % 2 == 0:
                    hb_ref[r0 + t * BATCH:r0 + (t + 2) * BATCH, cols] = (
                        jnp.concatenate([hcur[h], h_later], axis=0).astype(BF16))

    for rb in reversed(range(N_RB)):
        project(rb)
        if rb + 1 < N_RB:
            recur(rb + 1)
    recur(0)

    for h in range(LRU_HEADS):
        cols = slice(h * HEAD_DIM, (h + 1) * HEAD_DIM)
        carry_s[:, cols] = hcur[h]
        hfin_ref[:, cols] = hcur[h]


def _pre_bwd_call(x, mod, norm_g, w_all, conv_w, conv_b, w_gate, ba, bx, lam, h0, latent):
    _, seq, _ = x.shape
    n = seq // TL
    mod_rows = 0 if latent else 1
    tile = lambda j: (n - 1 - j, 0)
    out_shape = [jax.ShapeDtypeStruct((seq * BATCH, D_MODEL), BF16),
                 jax.ShapeDtypeStruct((BATCH, D_MODEL), F32)]
    out_specs = [pl.BlockSpec((RT, D_MODEL), tile),
                 pl.BlockSpec((BATCH, D_MODEL), lambda j: (0, 0))]
    if latent:
        out_shape += [jax.ShapeDtypeStruct((seq * BATCH, D_MODEL), BF16),
                      jax.ShapeDtypeStruct((seq * BATCH, D_MODEL), BF16)]
        out_specs += [pl.BlockSpec((RT, D_MODEL), tile), pl.BlockSpec((RT, D_MODEL), tile)]
    return pl.pallas_call(
        functools.partial(_pre_bwd_kernel, latent),
        grid=(n,),
        in_specs=[
            pl.BlockSpec(memory_space=pl.ANY),
            _mod_spec(mod_rows, 0),
            _mod_spec(mod_rows, 1),
            _const_spec((1, D_MODEL)),
            _col_spec(D_MODEL, 0, 2 if latent else 1),
            _const_spec((4, D_MODEL)),
            _const_spec((1, D_MODEL)),
            _const_spec((LRU_HEADS, HEAD_DIM, 2 * HEAD_DIM)),
            _const_spec((1, D_MODEL)),
            _const_spec((1, D_MODEL)),
            _const_spec((1, D_MODEL)),
            _const_spec((BATCH, D_MODEL)),
        ],
        out_specs=out_specs,
        out_shape=out_shape,
        scratch_shapes=[
            pltpu.VMEM((2, TL + HALO_T, BATCH, D_MODEL), F32),
            pltpu.SemaphoreType.DMA((2,)),
            pltpu.VMEM((RT + HALO_ROWS, D_MODEL), BF16),
            pltpu.VMEM((RT + 3 * BATCH, D_MODEL), F32),
            pltpu.VMEM((BATCH, D_MODEL), F32),
        ],
        compiler_params=pltpu.CompilerParams(dimension_semantics=("arbitrary",),
                                             vmem_limit_bytes=VMEM_LIMIT),
        name="lru_pre_bwd" if latent else "ctx_pre_bwd",
    )(x, mod, mod, norm_g, w_all, conv_w, conv_b, w_gate, ba, bx, lam, h0)


def _scan_kernel(combine, xc_ref, w_ref, ba_ref, bx_ref, lam_ref, h0_ref, *rest):
    if combine:
        hb_ref, sga_ref, hfin_ref, y_hbm, carry_s, a_s, u_s, h_s, ybuf, sem = rest
    else:
        hfin_ref, carry_s, a_s, u_s = rest

    j = pl.program_id(0)
    n = pl.num_programs(0)
    slot = j % 2

    @pl.when(j == 0)
    def _():
        carry_s[...] = h0_ref[...]

    def out_copies(tile, buf_slot):
        return [pltpu.make_async_copy(ybuf.at[buf_slot, :, b, :],
                                      y_hbm.at[b, pl.ds(tile * TL, TL), :],
                                      sem.at[buf_slot]) for b in range(BATCH)]

    k_all = _lru_coef(lam_ref[...])
    hba = 0.5 * ba_ref[...]
    hbx = 0.5 * bx_ref[...]
    for h in range(LRU_HEADS):
        cols = slice(h * HEAD_DIM, (h + 1) * HEAD_DIM)
        xb = xc_ref[:, cols]
        a, u = _lru_gates(xb, xb.astype(F32), w_ref[h], hba[:, cols], hbx[:, cols], k_all[:, cols])
        a_s[:, cols] = a
        u_s[:, cols] = u

    def step(kk, hcur):
        row = pl.multiple_of(kk * BATCH, BATCH)
        hnew = a_s[pl.ds(row, BATCH), :] * hcur + u_s[pl.ds(row, BATCH), :]
        if combine:
            h_s[pl.ds(row, BATCH), :] = hnew
        return hnew

    hlast = lax.fori_loop(0, TL, step, carry_s[...], unroll=8)
    carry_s[...] = hlast
    hfin_ref[...] = hlast

    if combine:
        @pl.when(j >= 2)
        def _():
            for cp in out_copies(j - 2, slot):
                cp.wait()

        y = (h_s[...] + hb_ref[...].astype(F32)) * sga_ref[...].astype(F32)
        ybuf[slot] = y.reshape(TL, BATCH, D_MODEL)
        for cp in out_copies(j, slot):
            cp.start()

        @pl.when(j == n - 1)
        def _():
            for cp in out_copies(j, slot):
                cp.wait()

        @pl.when(jnp.logical_and(j == n - 1, j >= 1))
        def _():
            for cp in out_copies(j - 1, 1 - slot):
                cp.wait()


def _scan_call(xc, w, ba, bx, lam, h0, hb=None, sga=None, *, combine):
    rows = xc.shape[0]
    seq = rows // BATCH
    n = seq // TL
    tile = lambda j: (j, 0)
    in_specs = [
        pl.BlockSpec((RT, D_MODEL), tile),
        _const_spec((LRU_HEADS, HEAD_DIM, 2 * HEAD_DIM)),
        _const_spec((1, D_MODEL)),
        _const_spec((1, D_MODEL)),
        _const_spec((1, D_MODEL)),
        _const_spec((BATCH, D_MODEL)),
    ]
    args = [xc, w, ba, bx, lam, h0]
    if combine:
        in_specs += [pl.BlockSpec((RT, D_MODEL), tile), pl.BlockSpec((RT, D_MODEL), tile)]
        args += [hb, sga]
    out_shape = [jax.ShapeDtypeStruct((BATCH, D_MODEL), F32)]
    out_specs = [pl.BlockSpec((BATCH, D_MODEL), lambda j: (0, 0))]
    if combine:
        out_shape.append(jax.ShapeDtypeStruct((BATCH, seq, D_MODEL), F32))
        out_specs.append(pl.BlockSpec(memory_space=pl.ANY))
    scratch = [pltpu.VMEM((BATCH, D_MODEL), F32),
               pltpu.VMEM((RT, D_MODEL), F32),
               pltpu.VMEM((RT, D_MODEL), F32)]
    if combine:
        scratch.append(pltpu.VMEM((RT, D_MODEL), F32))
        scratch.append(pltpu.VMEM((2, TL, BATCH, D_MODEL), F32))
        scratch.append(pltpu.SemaphoreType.DMA((2,)))
    return pl.pallas_call(
        functools.partial(_scan_kernel, combine),
        grid=(n,),
        in_specs=in_specs,
        out_specs=out_specs,
        out_shape=out_shape,
        scratch_shapes=scratch,
        compiler_params=pltpu.CompilerParams(dimension_semantics=("arbitrary",),
                                             vmem_limit_bytes=VMEM_LIMIT),
        name="lru_scan_fwd" + ("_combine" if combine else ""),
    )(*args)


def _final_kernel(x_ref, sh_ref, sc_ref, gx_ref, g_ref, wu_ref, wv_ref, wgb_ref, lng_ref, lnb_ref,
                  ws_ref, bs_ref, yl_ref, wo_ref, fg_ref, o_ref, vn_s, y_s):
    x = x_ref[...]
    gain = g_ref[...] * (1.0 + sc_ref[...])
    hn = _rms_scale(x) * gain[:, None, :] + sh_ref[...][:, None, :]
    lhs = hn.reshape(RT, D_MODEL).astype(BF16)

    v = _gelu_tanh(jnp.dot(lhs, wv_ref[...], preferred_element_type=F32))
    mu = jnp.mean(v, axis=-1, keepdims=True)
    vc = v - mu
    var = jnp.mean(vc * vc, axis=-1, keepdims=True)
    vn_s[...] = (vc * lax.rsqrt(var + LN_EPS) * lng_ref[...] + lnb_ref[...]).astype(BF16)

    u = _gelu_tanh(jnp.dot(lhs, wu_ref[...], preferred_element_type=F32))
    gb = _silu(jnp.dot(lhs, wgb_ref[...], preferred_element_type=F32))
    ug = u * gb

    y_s[:, 0:D_MODEL] = yl_ref[...].reshape(RT, D_MODEL).astype(BF16)
    for gidx in range(SGU_GROUPS):
        cols = slice(gidx * LANES, (gidx + 1) * LANES)
        rhs = jnp.concatenate([vn_s[b * TL:(b + 1) * TL, cols] for b in range(BATCH)], axis=1)
        mixed = jnp.dot(ws_ref[gidx], rhs, preferred_element_type=F32)
        for b in range(BATCH):
            rows = slice(b * TL, (b + 1) * TL)
            m_b = mixed[:, b * LANES:(b + 1) * LANES] + bs_ref[gidx]
            y_s[rows, D_MODEL + gidx * LANES:D_MODEL + (gidx + 1) * LANES] = (
                ug[rows, cols] * m_b).astype(BF16)

    mix = jnp.dot(y_s[...], wo_ref[...], preferred_element_type=F32)
    xnew = x + gx_ref[...][:, None, :] * mix.reshape(BATCH, TL, D_MODEL)
    o_ref[...] = _rms_scale(xnew) * fg_ref[...]


def _final_call(x, mod, norm_g, w_all, ln_g, ln_b, w_s, b_s, ylg, w_out, final_g):
    _, seq, _ = x.shape
    n = seq // TL
    tile = pl.BlockSpec((BATCH, TL, D_MODEL), lambda j: (0, j, 0))
    return pl.pallas_call(
        _final_kernel,
        grid=(n,),
        in_specs=[
            tile,
            _mod_spec(0, 0),
            _mod_spec(0, 1),
            _mod_spec(0, 2),
            _const_spec((1, D_MODEL)),
            _col_spec(D_MODEL, 2),
            _col_spec(D_MODEL, 3),
            _col_spec(D_MODEL, 4),
            _const_spec((1, D_MODEL)),
            _const_spec((1, D_MODEL)),
            _const_spec((SGU_GROUPS, CHUNK, CHUNK)),
            _const_spec((SGU_GROUPS, CHUNK, LANES)),
            tile,
            _const_spec(w_out.shape),
            _const_spec((1, D_MODEL)),
        ],
        out_specs=tile,
        out_shape=jax.ShapeDtypeStruct(x.shape, F32),
        scratch_shapes=[pltpu.VMEM((RT, D_MODEL), BF16),
                        pltpu.VMEM((RT, 2 * D_MODEL), BF16)],
        compiler_params=pltpu.CompilerParams(dimension_semantics=("arbitrary",),
                                             vmem_limit_bytes=VMEM_LIMIT),
        name="sgu_out_final",
    )(x, mod, mod, mod, norm_g, w_all, w_all, w_all, ln_g, ln_b, w_s, b_s, ylg, w_out, final_g)


def kernel(x, c, ctx, c_ctx, ada_w, ada_b, norm_g, w_in, conv_w, conv_b, lru_wa, lru_ba,
           lru_wx, lru_bx, lru_lambda, sgu_ln_g, sgu_ln_b, sgu_w, sgu_b, w_out, final_g):
    assert x.shape == (BATCH, x.shape[1], D_MODEL) and x.shape[1] % TL == 0
    assert ctx.shape == (BATCH, ctx.shape[1], D_MODEL) and ctx.shape[1] % TL == 0
    assert ada_w.shape[0] == 1, "single-layer block"
    layer = 0
    W = D_MODEL

    cond = jnp.concatenate([c, c_ctx[None, :], jnp.zeros((BATCH - 1, D_MODEL), F32)], axis=0)
    mod = _ada_call(cond, ada_w[layer], ada_b[layer][None, :])

    ng = norm_g[layer][None, :]
    w_all = w_in[layer].astype(BF16)
    cw = conv_w[layer]
    cb = conv_b[layer][None, :]
    w_gate = jnp.concatenate([lru_wa[layer], lru_wx[layer]], axis=-1).astype(BF16)
    ba = lru_ba[layer].reshape(2, 1, W)
    bx = lru_bx[layer].reshape(2, 1, W)
    lam = lru_lambda[layer].reshape(2, 1, W)
    zeros = jnp.zeros((BATCH, W), F32)
    FWD, BWD = 0, 1

    def pre_bwd(inp, h0, latent):
        return _pre_bwd_call(inp, mod, ng, w_all, cw, cb, w_gate[BWD], ba[BWD], bx[BWD], lam[BWD],
                             h0, latent)

    def scan_fwd(xc, h0, **kw):
        return _scan_call(xc, w_gate[FWD], ba[FWD], bx[FWD], lam[FWD], h0, **kw)

    xc_c, hb_c = pre_bwd(ctx, zeros, latent=False)
    (hf_c,) = scan_fwd(xc_c, zeros, combine=False)

    xc, _, hb, sga = pre_bwd(x, hb_c, latent=True)
    _, ylg = scan_fwd(xc, hf_c, hb=hb, sga=sga, combine=True)

    bs = jnp.broadcast_to(sgu_b[layer][:, :, None], (SGU_GROUPS, CHUNK, LANES))
    return _final_call(x, mod, ng, w_all, sgu_ln_g[layer][None, :], sgu_ln_b[layer][None, :],
                       sgu_w[layer].astype(BF16), bs, ylg, w_out[layer].astype(BF16),
                       final_g[None, :])
```
